```python
import math
import jax, jax.numpy as jnp
from jax import lax
import numpy as np

D_MODEL = 1024
BATCH = 4
SEQ = 4096
DEPTH = 4
DEC_BATCH = 32
DEC_SEQ = 8
PAST_LEN = 8192
PAGE_SIZE = 128

H_A = 4
DH_A = 128
Q_BLOCK = 128
H_B = 4
DK_B = 128
DV_B = 128
CONV_W = 4
DN_CONV_DIM = H_B * (2 * DK_B + DV_B)
H_C = 4
DK_C = 128
DV_C = 128
CHUNK = 64
D_FF = -(-8 * D_MODEL // (3 * 256)) * 256
EPS = 1e-6

SPLITS = (
    H_A * DH_A, H_A * DH_A, H_A * DH_A,
    DN_CONV_DIM,
    H_B, H_B, H_B * DV_B,
    H_C * DK_C, H_C * DK_C, H_C * DV_C, H_C * DV_C,
    D_MODEL, D_MODEL, D_MODEL,
)
D_IN = sum(SPLITS)
SPLIT_POINTS = tuple(sum(SPLITS[:i + 1]) for i in range(len(SPLITS) - 1))

kernel_name = 'hybrid_sb_deltanet_hgrn2_decoder_step'


def _rmsnorm(x, g):
    xf = x.astype(jnp.float32)
    y = xf * lax.rsqrt(jnp.mean(xf * xf, axis=-1, keepdims=True) + EPS)
    return (y * g.astype(jnp.float32)).astype(x.dtype)


def _l2norm(x):
    return x * lax.rsqrt(jnp.sum(x * x, axis=-1, keepdims=True) + EPS)


def _chunk_len(T):
    return CHUNK if T % CHUNK == 0 else T


def _to_chunks(a, C):
    B, T = a.shape[:2]
    a = a.reshape((B, T // C, C) + a.shape[2:])
    return jnp.moveaxis(jnp.moveaxis(a, 1, 0), 3, 2)


def _from_chunks(o):
    n, B, H, C, d = o.shape
    return jnp.moveaxis(jnp.moveaxis(o, 2, 3), 0, 1).reshape(B, n * C, H, d)


def _stick_breaking(q, k, v, bias, q_pos, k_pos):
    z = (jnp.einsum('bqhd,bkhd->bhqk', q.astype(jnp.float32), k.astype(jnp.float32)) * (DH_A ** -0.5)
         + bias.astype(jnp.float32)[None, :, None, None])
    valid = k_pos[None, :] < q_pos[:, None]
    log_1mb = jnp.where(valid, jax.nn.log_sigmoid(-z), 0.0)
    suffix = lax.cumsum(log_1mb, axis=3, reverse=True) - log_1mb
    w = jnp.where(valid, jnp.exp(jax.nn.log_sigmoid(z) + suffix), 0.0)
    return jnp.einsum('bhqk,bkhd->bqhd', w, v.astype(jnp.float32))


def _stick_breaking_prompt(q, k, v, bias):
    B, T, H, D = q.shape
    nb = T // Q_BLOCK
    qb = jnp.moveaxis(q.reshape(B, nb, Q_BLOCK, H, D), 1, 0)
    pos = jnp.arange(T)
    qpos = pos.reshape(nb, Q_BLOCK)
    out = lax.map(lambda a: _stick_breaking(a[0], k, v, bias, a[1], pos), (qb, qpos))
    return jnp.moveaxis(out, 0, 1).reshape(B, T, H, D)


def _gated_delta_rule(q, k, v, beta, g, S0):
    C = _chunk_len(q.shape[1])
    dv = v.shape[-1]
    incl = jnp.tril(jnp.ones((C, C), bool))
    strict = jnp.tril(jnp.ones((C, C), bool), -1)
    eye = jnp.eye(C, dtype=jnp.float32)

    def step(S, inp):
        qc, kc, vc, bc, gc = inp
        gcum = jnp.cumsum(gc, axis=-1)
        decay = jnp.exp(jnp.where(incl, gcum[..., :, None] - gcum[..., None, :], -jnp.inf))
        kb = kc * bc[..., None]
        m = eye + jnp.where(strict, jnp.einsum('bhid,bhjd->bhij', kb, kc) * decay, 0.0)
        rhs = jnp.concatenate([vc * bc[..., None], kb * jnp.exp(gcum)[..., None]], axis=-1)
        sol = lax.linalg.triangular_solve(m, rhs, left_side=True, lower=True, unit_diagonal=True)
        u, w = sol[..., :dv], sol[..., dv:]
        v_new = u - jnp.einsum('bhck,bhkv->bhcv', w, S)
        attn = jnp.einsum('bhid,bhjd->bhij', qc, kc) * decay
        o = (jnp.einsum('bhck,bhkv->bhcv', qc * jnp.exp(gcum)[..., None], S)
             + jnp.einsum('bhij,bhjv->bhiv', attn, v_new))
        g_last = gcum[..., -1:]
        S = (S * jnp.exp(g_last)[..., None]
             + jnp.einsum('bhck,bhcv->bhkv', kc * jnp.exp(g_last - gcum)[..., None], v_new))
        return S, o

    S, o = lax.scan(step, S0, tuple(_to_chunks(a, C) for a in (q, k, v, beta, g)))
    return _from_chunks(o), S


def _gated_linear_recurrence(q, k, v, logf, S0):
    C = _chunk_len(q.shape[1])
    incl = jnp.tril(jnp.ones((C, C), bool))

    def step(S, inp):
        qc, kc, vc, lf = inp
        gcum = jnp.cumsum(lf, axis=2)
        diff = gcum[:, :, :, None, :] - gcum[:, :, None, :, :]
        decay = jnp.exp(jnp.where(incl[:, :, None], diff, -jnp.inf))
        attn = jnp.einsum('bhid,bhjd,bhijd->bhij', qc, kc, decay)
        o = (jnp.einsum('bhcd,bhdv->bhcv', qc * jnp.exp(gcum), S)
             + jnp.einsum('bhij,bhjv->bhiv', attn, vc))
        g_last = gcum[:, :, -1:, :]
        S = (S * jnp.exp(g_last[:, :, 0, :, None])
             + jnp.einsum('bhcd,bhcv->bhdv', kc * jnp.exp(g_last - gcum), vc))
        return S, o

    S, o = lax.scan(step, S0, tuple(_to_chunks(a, C) for a in (q, k, v, logf)))
    return _from_chunks(o), S


def _deltanet_branch(u_qkv, a_in, b_in, z, conv_buf, S0, conv_w, A_log, dt_bias, norm_g):
    B, T, _ = u_qkv.shape
    xpad = jnp.concatenate([conv_buf.astype(u_qkv.dtype), u_qkv], axis=1)
    acc = xpad[:, 0:T] * conv_w[0]
    for i in range(1, CONV_W):
        acc = acc + xpad[:, i:i + T] * conv_w[i]
    y = jax.nn.silu(acc).astype(jnp.float32)
    q, k, v = jnp.split(y, [H_B * DK_B, 2 * H_B * DK_B], axis=-1)
    q = _l2norm(q.reshape(B, T, H_B, DK_B)) * (DK_B ** -0.5)
    k = _l2norm(k.reshape(B, T, H_B, DK_B))
    v = v.reshape(B, T, H_B, DV_B)
    beta = jax.nn.sigmoid(b_in.astype(jnp.float32))
    g = -jnp.exp(A_log.astype(jnp.float32)) * jax.nn.softplus(a_in.astype(jnp.float32) + dt_bias.astype(jnp.float32))
    o, S = _gated_delta_rule(q, k, v, beta, g, S0.astype(jnp.float32))
    o = _rmsnorm(o, norm_g) * jax.nn.silu(z.astype(jnp.float32).reshape(B, T, H_B, DV_B))
    return o.reshape(B, T, H_B * DV_B), xpad[:, T:], S


def _hgrn2_branch(q_in, f_in, i_in, z, S0, lb, norm_g):
    B, T, _ = q_in.shape
    sh = (B, T, H_C, DK_C)
    q = jax.nn.silu(q_in.astype(jnp.float32)).reshape(sh) * (DK_C ** -0.5)
    fg = (lb + (1.0 - lb) * jax.nn.sigmoid(f_in.astype(jnp.float32))).reshape(sh)
    v = i_in.astype(jnp.float32).reshape(B, T, H_C, DV_C)
    o, S = _gated_linear_recurrence(q, 1.0 - fg, v, jnp.log(fg), S0.astype(jnp.float32))
    o = _rmsnorm(o, norm_g) * jax.nn.silu(z.astype(jnp.float32).reshape(B, T, H_C, DV_C))
    return o.reshape(B, T, H_C * DV_C), S


def _layer(x, l, p, sb_past, dn_buf, dn_S0, hg_S0):
    B, T, _ = x.shape
    h = _rmsnorm(x, p['ln1_g'][l])
    (sb_q, sb_k, sb_v, dn_qkv, dn_a, dn_b, dn_z, hg_q, hg_f, hg_i, hg_z,
     gate_a, gate_b, gate_c) = jnp.split(h @ p['w_in'][l], SPLIT_POINTS, axis=-1)

    q_a = sb_q.reshape(B, T, H_A, DH_A)
    k_a = sb_k.reshape(B, T, H_A, DH_A)
    v_a = sb_v.reshape(B, T, H_A, DH_A)
    bias = p['sb_bias'][l]
    if sb_past is None:
        o_a = _stick_breaking_prompt(q_a, k_a, v_a, bias)
    else:
        pk, pv = sb_past
        past_len = pk.shape[1]
        o_a = _stick_breaking(q_a, jnp.concatenate([pk, k_a], axis=1), jnp.concatenate([pv, v_a], axis=1),
                              bias, past_len + jnp.arange(T), jnp.arange(past_len + T))
    o_a = o_a.reshape(B, T, H_A * DH_A).astype(x.dtype)

    o_b, dn_buf_new, dn_S = _deltanet_branch(dn_qkv, dn_a, dn_b, dn_z, dn_buf, dn_S0, p['dn_conv_w'][l],
                                             p['dn_A_log'][l], p['dn_dt_bias'][l], p['dn_norm_g'][l])
    o_c, hg_S = _hgrn2_branch(hg_q, hg_f, hg_i, hg_z, hg_S0, p['hg_lb'][l], p['hg_norm_g'][l])

    mixed = (jax.nn.sigmoid(gate_a) * (o_a @ p['w_br_a'][l])
             + jax.nn.sigmoid(gate_b) * (o_b.astype(x.dtype) @ p['w_br_b'][l])
             + jax.nn.sigmoid(gate_c) * (o_c.astype(x.dtype) @ p['w_br_c'][l]))
    x = x + mixed @ p['w_o'][l]

    h2 = _rmsnorm(x, p['ln2_g'][l])
    gg, uu = jnp.split(h2 @ p['w_ffn_in'][l], 2, axis=-1)
    x = x + (jax.nn.silu(gg) * uu) @ p['w_ffn_out'][l]
    return x, k_a, v_a, dn_buf_new, dn_S, hg_S


def _trunk(x, p, page_table, cache_k, cache_v, dn_conv, dn_S, hg_S):
    B = x.shape[0]
    ks, vs, bufs, dSs, hSs = [], [], [], [], []
    for l in range(DEPTH):
        if page_table is None:
            sb_past = None
            buf0 = jnp.zeros((B, CONV_W - 1, DN_CONV_DIM), x.dtype)
            Sb0 = jnp.zeros((B, H_B, DK_B, DV_B), jnp.float32)
            Sc0 = jnp.zeros((B, H_C, DK_C, DV_C), jnp.float32)
        else:
            nb, npg = page_table.shape
            pk = cache_k[l][page_table].reshape(nb, npg * PAGE_SIZE, H_A, DH_A)
            pv = cache_v[l][page_table].reshape(nb, npg * PAGE_SIZE, H_A, DH_A)
            sb_past = (pk, pv)
            buf0, Sb0, Sc0 = dn_conv[l], dn_S[l], hg_S[l]
        x, k_new, v_new, buf, Sb, Sc = _layer(x, l, p, sb_past, buf0, Sb0, Sc0)
        ks.append(k_new)
        vs.append(v_new)
        bufs.append(buf)
        dSs.append(Sb)
        hSs.append(Sc)
    y = _rmsnorm(x, p['lnf_g'])
    return y, jnp.stack(ks), jnp.stack(vs), jnp.stack(bufs), jnp.stack(dSs), jnp.stack(hSs)


def setup_inputs(seed: int = 0) -> dict:
    key = jax.random.key(seed)
    ks = jax.random.split(key, 32)
    n_pages = PAST_LEN // PAGE_SIZE
    n_pool = (DEC_BATCH * n_pages * 5) // 4

    def nrm(k, shape, s):
        return jax.random.normal(k, shape, jnp.float32) * s

    def gain(k, shape):
        return 1.0 + 0.02 * jax.random.normal(k, shape, jnp.float32)

    page_table = jax.random.permutation(ks[7], n_pool)[:DEC_BATCH * n_pages].reshape(DEC_BATCH, n_pages).astype(jnp.int32)
    dt = jnp.exp(jax.random.uniform(ks[12], (DEPTH, H_B), jnp.float32, math.log(1e-3), math.log(0.1)))
    return {
        'x_prompt': nrm(ks[0], (BATCH, SEQ, D_MODEL), 1.0),
        'x_sample': nrm(ks[1], (DEC_BATCH, DEC_SEQ, D_MODEL), 1.0),
        'cache_sb_k': nrm(ks[2], (DEPTH, n_pool, PAGE_SIZE, H_A, DH_A), 1.0),
        'cache_sb_v': nrm(ks[3], (DEPTH, n_pool, PAGE_SIZE, H_A, DH_A), 1.0),
        'state_dn_conv': nrm(ks[4], (DEPTH, DEC_BATCH, CONV_W - 1, DN_CONV_DIM), 1.0),
        'state_dn_S': nrm(ks[5], (DEPTH, DEC_BATCH, H_B, DK_B, DV_B), 0.5),
        'state_hg_S': nrm(ks[6], (DEPTH, DEC_BATCH, H_C, DK_C, DV_C), 0.5),
        'page_table': page_table,
        'ln1_g': gain(ks[8], (DEPTH, D_MODEL)),
        'w_in': nrm(ks[9], (DEPTH, D_MODEL, D_IN), D_MODEL ** -0.5),
        'sb_bias': jax.random.uniform(ks[24], (DEPTH, H_A), jnp.float32, -8.0, -6.0),
        'dn_conv_w': nrm(ks[10], (DEPTH, CONV_W, DN_CONV_DIM), CONV_W ** -0.5),
        'dn_A_log': jnp.log(jax.random.uniform(ks[11], (DEPTH, H_B), jnp.float32, 1.0, 16.0)),
        'dn_dt_bias': dt + jnp.log(-jnp.expm1(-dt)),
        'dn_norm_g': gain(ks[13], (DEPTH, DV_B)),
        'hg_lb_logits': nrm(ks[14], (DEPTH, H_C * DK_C), 0.5),
        'hg_norm_g': gain(ks[15], (DEPTH, DV_C)),
        'w_br_a': nrm(ks[16], (DEPTH, H_A * DH_A, D_MODEL), (H_A * DH_A) ** -0.5),
        'w_br_b': nrm(ks[17], (DEPTH, H_B * DV_B, D_MODEL), (H_B * DV_B) ** -0.5),
        'w_br_c': nrm(ks[18], (DEPTH, H_C * DV_C, D_MODEL), (H_C * DV_C) ** -0.5),
        'w_o': nrm(ks[19], (DEPTH, D_MODEL, D_MODEL), D_MODEL ** -0.5),
        'ln2_g': gain(ks[20], (DEPTH, D_MODEL)),
        'w_ffn_in': nrm(ks[21], (DEPTH, D_MODEL, 2 * D_FF), D_MODEL ** -0.5),
        'w_ffn_out': nrm(ks[22], (DEPTH, D_FF, D_MODEL), D_FF ** -0.5),
        'lnf_g': gain(ks[23], (D_MODEL,)),
    }


def reference(x_prompt, x_sample, cache_sb_k, cache_sb_v, state_dn_conv, state_dn_S, state_hg_S, page_table,
              ln1_g, w_in, sb_bias, dn_conv_w, dn_A_log, dn_dt_bias, dn_norm_g, hg_lb_logits, hg_norm_g,
              w_br_a, w_br_b, w_br_c, w_o, ln2_g, w_ffn_in, w_ffn_out, lnf_g):
    lb_p = jax.nn.softmax(hg_lb_logits.astype(jnp.float32), axis=0)
    hg_lb = jnp.cumsum(lb_p, axis=0) - lb_p[0:1]
    p = dict(ln1_g=ln1_g, w_in=w_in, sb_bias=sb_bias, dn_conv_w=dn_conv_w, dn_A_log=dn_A_log,
             dn_dt_bias=dn_dt_bias, dn_norm_g=dn_norm_g, hg_lb=hg_lb, hg_norm_g=hg_norm_g, w_br_a=w_br_a,
             w_br_b=w_br_b, w_br_c=w_br_c, w_o=w_o, ln2_g=ln2_g, w_ffn_in=w_ffn_in, w_ffn_out=w_ffn_out,
             lnf_g=lnf_g)
    y_prompt, sbk_p, sbv_p, conv_p, dnS_p, hgS_p = _trunk(x_prompt, p, None, None, None, None, None, None)
    y_sample, sbk_s, sbv_s, conv_s, dnS_s, hgS_s = _trunk(x_sample, p, page_table, cache_sb_k, cache_sb_v,
                                                          state_dn_conv, state_dn_S, state_hg_S)
    return (y_prompt, y_sample, sbk_p, sbv_p, sbk_s, sbv_s, conv_p, conv_s, dnS_p, dnS_s, hgS_p, hgS_s)
```

```python
import functools

import jax
import jax.numpy as jnp
from jax import lax
from jax.experimental import pallas as pl
from jax.experimental.pallas import tpu as pltpu

F32 = jnp.float32
BF16 = jnp.bfloat16
EPS = 1e-6

N_HEADS = 4
HEAD_DIM = 128
HD = N_HEADS * HEAD_DIM
CONV_W = 4
PAGE = 128
CHUNK = 64
SUB = 16
LANE = 128
VMEM_LIMIT = 56 * 1024 * 1024

C_SBQ, C_SBK, C_SBV = 0, 512, 1024
C_DNQKV = 1536
C_GATE = 3072
C_DNZ = 6144
C_HGQ, C_HGF, C_HGI, C_HGZ = 6656, 7168, 7680, 8192
C_AB = 8704
N_PROJ = 8960
PROJ_TN = 1280


def _sigmoid(x):
    return 1.0 / (1.0 + jnp.exp(-x))


def _silu(x):
    return x * _sigmoid(x)


def _softplus(x):
    return jnp.maximum(x, 0.0) + jnp.log1p(jnp.exp(-jnp.abs(x)))


def _dot(a, b):
    return jnp.dot(a.astype(BF16), b.astype(BF16), preferred_element_type=F32)


def _dot_nt(a, b):
    return lax.dot_general(a.astype(BF16), b.astype(BF16), (((1,), (1,)), ((), ())),
                           preferred_element_type=F32)


def _dot_tn(a, b):
    return lax.dot_general(a.astype(BF16), b.astype(BF16), (((0,), (0,)), ((), ())),
                           preferred_element_type=F32)


def _split(a, n):
    out = []
    r = a
    for t in range(n):
        p = r.astype(BF16)
        out.append(p)
        if t + 1 < n:
            r = r - p.astype(F32)
    return out


def _dot_exact_rhs(a, m_bf16, n=3):
    acc = None
    for p in _split(a, n):
        t = jnp.dot(p, m_bf16, preferred_element_type=F32)
        acc = t if acc is None else acc + t
    return acc


def _dot_exact_lhs(m_bf16, a, n=3):
    acc = None
    for p in _split(a, n):
        t = jnp.dot(m_bf16, p, preferred_element_type=F32)
        acc = t if acc is None else acc + t
    return acc


def _dot_hp(a, b, nt=False):
    a1, a2 = _split(a, 2)
    b1, b2 = _split(b, 2)
    if nt:
        f = lambda x, y: lax.dot_general(x, y, (((1,), (1,)), ((), ())), preferred_element_type=F32)
    else:
        f = lambda x, y: jnp.dot(x, y, preferred_element_type=F32)
    return f(a1, b1) + (f(a1, b2) + f(a2, b1))


def _iota2(shape):
    return (lax.broadcasted_iota(jnp.int32, shape, 0), lax.broadcasted_iota(jnp.int32, shape, 1))


def _log2(n):
    k = n.bit_length() - 1
    assert (1 << k) == n
    return k


def _cparams(sem):
    return pltpu.CompilerParams(dimension_semantics=sem, vmem_limit_bytes=VMEM_LIMIT)


def _inproj_kernel(x_ref, g_ref, w_ref, o_ref, h_scr):
    @pl.when(pl.program_id(1) == 0)
    def _():
        x = x_ref[...]
        ms = jnp.mean(x * x, axis=-1, keepdims=True)
        h_scr[...] = (x * lax.rsqrt(ms + EPS) * g_ref[...]).astype(BF16)

    o_ref[...] = jnp.dot(h_scr[...], w_ref[...], preferred_element_type=F32)


def _inproj(x2d, g_all, w_all, l, tm):
    M, D = x2d.shape
    return pl.pallas_call(
        _inproj_kernel,
        grid=(M // tm, N_PROJ // PROJ_TN),
        in_specs=[
            pl.BlockSpec((tm, D), lambda i, j: (i, 0)),
            pl.BlockSpec((None, 1, D), lambda i, j: (l, 0, 0)),
            pl.BlockSpec((None, D, PROJ_TN), lambda i, j: (l, 0, j)),
        ],
        out_specs=pl.BlockSpec((tm, PROJ_TN), lambda i, j: (i, j)),
        out_shape=jax.ShapeDtypeStruct((M, N_PROJ), F32),
        scratch_shapes=[pltpu.VMEM((tm, D), BF16)],
        compiler_params=_cparams(("parallel", "arbitrary")),
        name="inproj",
    )(x2d, g_all, w_all)


def _sb_block(q, k, v, bias, U, acc_ref, csum_ref, valid):
    z = _dot_nt(q, k) + bias
    lg = -_softplus(z)
    if valid is not None:
        lg = jnp.where(valid, lg, 0.0)
    hi, lo = _split(lg, 2)
    suffix = jnp.dot(hi, U, preferred_element_type=F32) + jnp.dot(lo, U, preferred_element_type=F32)
    w = jnp.exp(z + lg + suffix + csum_ref[...])
    if valid is not None:
        w = jnp.where(valid, w, 0.0)
    acc_ref[...] += _dot(w, v)
    csum_ref[...] += jnp.sum(lg, axis=-1, keepdims=True)


def _sbp_kernel(bias_ref, q_ref, k_ref, v_ref, u_ref, o_ref, kbf, vbf, acc, csum, *, tq):
    h = pl.program_id(1)
    i = pl.program_id(2)

    @pl.when(i == 0)
    def _():
        kbf[...] = k_ref[...].astype(BF16)
        vbf[...] = v_ref[...].astype(BF16)

    bias = bias_ref[h]
    q = (q_ref[...] * (HEAD_DIM ** -0.5)).astype(BF16)
    U = u_ref[...]
    acc[...] = jnp.zeros_like(acc)
    csum[...] = jnp.zeros_like(csum)

    row, col = _iota2((tq, tq))
    start = pl.multiple_of(i * tq, tq)
    _sb_block(q, kbf[pl.ds(start, tq), :], vbf[pl.ds(start, tq), :], bias, U, acc, csum, col < row)

    def body(jj, carry):
        s = pl.multiple_of((i - 1 - jj) * tq, tq)
        _sb_block(q, kbf[pl.ds(s, tq), :], vbf[pl.ds(s, tq), :], bias, U, acc, csum, None)
        return carry

    lax.fori_loop(0, i, body, 0)
    o_ref[...] = acc[...]


def _strict_upper_sum_matrix(n):
    r = jnp.arange(n)
    return (r[:, None] > r[None, :]).astype(BF16)


def _sb_prompt(proj, bias_l, B, T):
    tq = min(256, T)
    nq = T // tq
    U = _strict_upper_sum_matrix(tq)
    return pl.pallas_call(
        functools.partial(_sbp_kernel, tq=tq),
        grid=(B, N_HEADS, nq),
        in_specs=[
            pl.BlockSpec(memory_space=pltpu.SMEM),
            pl.BlockSpec((tq, HEAD_DIM), lambda b, h, i: (b * nq + i, C_SBQ // HEAD_DIM + h)),
            pl.BlockSpec((T, HEAD_DIM), lambda b, h, i: (b, C_SBK // HEAD_DIM + h)),
            pl.BlockSpec((T, HEAD_DIM), lambda b, h, i: (b, C_SBV // HEAD_DIM + h)),
            pl.BlockSpec((tq, tq), lambda b, h, i: (0, 0)),
        ],
        out_specs=pl.BlockSpec((tq, HEAD_DIM), lambda b, h, i: (b * nq + i, h)),
        out_shape=jax.ShapeDtypeStruct((B * T, HD), F32),
        scratch_shapes=[
            pltpu.VMEM((T, HEAD_DIM), BF16),
            pltpu.VMEM((T, HEAD_DIM), BF16),
            pltpu.VMEM((tq, HEAD_DIM), F32),
            pltpu.VMEM((tq, 1), F32),
        ],
        compiler_params=_cparams(("parallel", "parallel", "arbitrary")),
        name="sb_prompt",
    )(bias_l, proj, proj, proj, U)


def _sbd_kernel(pt_ref, bias_ref, q_ref, kn_ref, vn_ref, *rest, G, n_groups, ts):
    kpages = rest[:G]
    vpages = rest[G:2 * G]
    u_ref, o_ref, qbd, kbuf, vbuf, acc, csum = rest[2 * G:]
    del pt_ref
    g = pl.program_id(1)
    rows = N_HEADS * ts
    U = u_ref[...]
    bias = bias_ref[...]

    @pl.when(g == 0)
    def _():
        acc[...] = jnp.zeros_like(acc)
        csum[...] = jnp.zeros_like(csum)
        q = q_ref[...] * (HEAD_DIM ** -0.5)
        lane_head = lax.broadcasted_iota(jnp.int32, (ts, HD), 1) // HEAD_DIM
        qbd[...] = jnp.concatenate(
            [jnp.where(lane_head == hh, q, 0.0) for hh in range(N_HEADS)], axis=0).astype(BF16)
        kbuf[...] = jnp.zeros_like(kbuf)
        vbuf[...] = jnp.zeros_like(vbuf)
        kbuf[0:ts, :] = kn_ref[...].astype(BF16)
        vbuf[0:ts, :] = vn_ref[...].astype(BF16)
        row, col = _iota2((rows, PAGE))
        valid = col < (row % ts)
        _sb_block(qbd[...], kbuf[...], vbuf[...], bias, U, acc, csum, valid)

    @pl.when(g > 0)
    def _():
        for p in range(G):
            _sb_block(qbd[...], kpages[p][...].astype(BF16), vpages[p][...].astype(BF16),
                      bias, U, acc, csum, None)

    @pl.when(g == n_groups)
    def _():
        for hh in range(N_HEADS):
            o_ref[:, hh * HEAD_DIM:(hh + 1) * HEAD_DIM] = acc[hh * ts:(hh + 1) * ts,
                                                             hh * HEAD_DIM:(hh + 1) * HEAD_DIM]


def _sb_decode(proj, cache_k, cache_v, page_table, bias_l, l, nb, ts):
    npg = page_table.shape[1]
    G = min(8, npg)
    n_groups = npg // G
    rows = N_HEADS * ts
    U = _strict_upper_sum_matrix(PAGE)
    bias_rows = jnp.broadcast_to(jnp.repeat(bias_l, ts)[:, None], (rows, PAGE)).astype(F32)

    def page_spec(p):
        def imap(b, g, pt):
            first = jnp.maximum(g - 1, 0) * G
            return (l, pt[b, npg - 1 - (first + p)], 0, 0)
        return pl.BlockSpec((None, None, PAGE, HD), imap)

    in_specs = [
        pl.BlockSpec((rows, PAGE), lambda b, g, pt: (0, 0)),
        pl.BlockSpec((ts, HD), lambda b, g, pt: (b, C_SBQ // HD)),
        pl.BlockSpec((ts, HD), lambda b, g, pt: (b, C_SBK // HD)),
        pl.BlockSpec((ts, HD), lambda b, g, pt: (b, C_SBV // HD)),
    ] + [page_spec(p) for p in range(G)] + [page_spec(p) for p in range(G)] + [
        pl.BlockSpec((PAGE, PAGE), lambda b, g, pt: (0, 0)),
    ]
    grid_spec = pltpu.PrefetchScalarGridSpec(
        num_scalar_prefetch=1,
        grid=(nb, n_groups + 1),
        in_specs=in_specs,
        out_specs=pl.BlockSpec((ts, HD), lambda b, g, pt: (b, 0)),
        scratch_shapes=[
            pltpu.VMEM((rows, HD), BF16),
            pltpu.VMEM((PAGE, HD), BF16),
            pltpu.VMEM((PAGE, HD), BF16),
            pltpu.VMEM((rows, HD), F32),
            pltpu.VMEM((rows, 1), F32),
        ],
    )
    return pl.pallas_call(
        functools.partial(_sbd_kernel, G=G, n_groups=n_groups, ts=ts),
        grid_spec=grid_spec,
        out_shape=jax.ShapeDtypeStruct((nb * ts, HD), F32),
        compiler_params=_cparams(("parallel", "arbitrary")),
        name="sb_decode",
    )(page_table, bias_rows, proj, proj, proj, *([cache_k] * G), *([cache_v] * G), U)


def _unit_lower_inverse(A, C):
    row, col = _iota2((C, C))
    eye = (row == col).astype(F32)
    blk = min(SUB, C)
    sh = _log2(blk)
    same = (row >> sh) == (col >> sh)
    Dg = jnp.where(same, A, 0.0)
    N = -Dg
    R = eye + N
    M = N
    for _ in range(sh - 1):
        M = _dot_hp(M, M)
        R = R + _dot_hp(R, M)
    if C > blk:
        Lo = jnp.where(same, 0.0, A)
        Mm = -_dot_hp(R, Lo)
        X = eye + Mm
        P = Mm
        for _ in range(_log2(C // blk) - 1):
            P = _dot_hp(P, P)
            X = X + _dot_hp(X, P)
        R = _dot_hp(X, R)
    return R


def _dn_kernel(qkv_ref, z_ref, ab_ref, abT_ref, conv0_ref, s0_ref, convw_ref, alog_row_ref,
               dtb_row_ref, alog_col_ref, dtb_col_ref, ng_ref, o_ref, sout_ref, xbuf, s_scr, *, C, nc):
    c = pl.program_id(1)

    @pl.when(c == 0)
    def _():
        xbuf[0:8, :] = conv0_ref[...]
        s_scr[...] = s0_ref[...]

    @pl.when(c > 0)
    def _():
        xbuf[0:8, :] = xbuf[C:C + 8, :]

    xbuf[8:8 + C, :] = qkv_ref[...]
    acc = xbuf[5:5 + C, :] * convw_ref[0:1, :]
    for t in range(1, CONV_W):
        acc = acc + xbuf[5 + t:5 + t + C, :] * convw_ref[t:t + 1, :]
    y = _silu(acc)

    row, col = _iota2((C, C))
    tri_incl_lower = (row >= col).astype(BF16)
    tri_incl_upper = (row <= col).astype(BF16)

    ab = ab_ref[...]
    g_col = -jnp.exp(alog_row_ref[...]) * _softplus(ab + dtb_row_ref[...])
    beta_all = _sigmoid(ab)
    gcum_col = _dot_exact_lhs(tri_incl_lower, g_col)
    abT = abT_ref[...]
    g_row = -jnp.exp(alog_col_ref[:, 0:C]) * _softplus(abT + dtb_col_ref[:, 0:C])
    gcum_row = _dot_exact_rhs(g_row, tri_incl_upper)

    ng = ng_ref[...]
    for h in range(N_HEADS):
        hs = slice(h * HEAD_DIM, (h + 1) * HEAD_DIM)
        q = y[:, hs]
        k = y[:, HD + h * HEAD_DIM:HD + (h + 1) * HEAD_DIM]
        v = y[:, 2 * HD + h * HEAD_DIM:2 * HD + (h + 1) * HEAD_DIM]
        q = q * lax.rsqrt(jnp.sum(q * q, axis=-1, keepdims=True) + EPS) * (HEAD_DIM ** -0.5)
        k = k * lax.rsqrt(jnp.sum(k * k, axis=-1, keepdims=True) + EPS)
        beta = beta_all[:, N_HEADS + h:N_HEADS + h + 1]
        gc = gcum_col[:, h:h + 1]
        gr = gcum_row[h:h + 1, :]
        decay = jnp.exp(jnp.where(row >= col, gc - gr, -jnp.inf))
        kb = k * beta
        A = jnp.where(row > col, _dot_hp(kb, k, nt=True) * decay, 0.0)
        Tm = _unit_lower_inverse(A, C)
        eg = jnp.exp(gc)
        sol = _dot_hp(Tm, jnp.concatenate([v * beta, kb * eg], axis=-1))
        u = sol[:, :HEAD_DIM]
        w = sol[:, HEAD_DIM:]
        S = s_scr[h]
        v_new = u - _dot(w, S)
        attn = _dot_nt(q, k) * decay
        o = _dot(q * eg, S) + _dot(attn, v_new)
        g_last = gc[C - 1:C, :]
        S_new = S * jnp.exp(g_last) + _dot_tn(k * jnp.exp(g_last - gc), v_new)
        s_scr[h] = S_new
        o = o * lax.rsqrt(jnp.mean(o * o, axis=-1, keepdims=True) + EPS) * ng
        o_ref[:, hs] = o * _silu(z_ref[:, hs])

        @pl.when(c == nc - 1)
        def _():
            sout_ref[h] = S_new


def _deltanet(proj, abT, conv0, S0, convw_l, alog_row, dtb_row, alog_col, dtb_col, ng, B, T):
    C = CHUNK if T % CHUNK == 0 else T
    nc = T // C
    W3 = 3 * HD
    const = lambda b, c: (0, 0)
    return pl.pallas_call(
        functools.partial(_dn_kernel, C=C, nc=nc),
        grid=(B, nc),
        in_specs=[
            pl.BlockSpec((C, W3), lambda b, c: (b * nc + c, C_DNQKV // W3)),
            pl.BlockSpec((C, HD), lambda b, c: (b * nc + c, C_DNZ // HD)),
            pl.BlockSpec((C, LANE), lambda b, c: (b * nc + c, C_AB // LANE)),
            pl.BlockSpec((None, None, 8, C), lambda b, c: (b, c, 0, 0)),
            pl.BlockSpec((None, 8, W3), lambda b, c: (b, 0, 0)),
            pl.BlockSpec((None, N_HEADS, HEAD_DIM, HEAD_DIM), lambda b, c: (b, 0, 0, 0)),
            pl.BlockSpec((CONV_W, W3), const),
            pl.BlockSpec((1, LANE), const),
            pl.BlockSpec((1, LANE), const),
            pl.BlockSpec((8, LANE), const),
            pl.BlockSpec((8, LANE), const),
            pl.BlockSpec((1, HEAD_DIM), const),
        ],
        out_specs=[
            pl.BlockSpec((C, HD), lambda b, c: (b * nc + c, 0)),
            pl.BlockSpec((None, N_HEADS, HEAD_DIM, HEAD_DIM), lambda b, c: (b, 0, 0, 0)),
        ],
        out_shape=[
            jax.ShapeDtypeStruct((B * T, HD), F32),
            jax.ShapeDtypeStruct((B, N_HEADS, HEAD_DIM, HEAD_DIM), F32),
        ],
        scratch_shapes=[
            pltpu.VMEM((8 + C, W3), F32),
            pltpu.VMEM((N_HEADS, HEAD_DIM, HEAD_DIM), F32),
        ],
        compiler_params=_cparams(("parallel", "arbitrary")),
        name="deltanet",
    )(proj, proj, proj, abT, conv0, S0, convw_l, alog_row, dtb_row, alog_col, dtb_col, ng)


def _hg_kernel(q_ref, f_ref, i_ref, z_ref, s0_ref, logit_ref, ng_ref, o_ref, sout_ref,
               kpad, gpad, vpad, st_scr, *, C, nc, layer):
    c = pl.program_id(1)
    sb = min(SUB, C)

    @pl.when(c == 0)
    def _():
        for h in range(N_HEADS):
            st_scr[h] = s0_ref[h].T
        kpad[0:sb, :] = jnp.zeros((sb, HD), F32)
        gpad[0:sb, :] = jnp.zeros((sb, HD), F32)
        vpad[0:sb, :] = jnp.zeros((sb, HD), F32)

    logits = logit_ref[...]
    e = jnp.exp(logits - jnp.max(logits, axis=0, keepdims=True))
    p = e / jnp.sum(e, axis=0, keepdims=True)
    lb = jnp.sum(p[0:layer + 1, :], axis=0, keepdims=True) - p[0:1, :]

    q = _silu(q_ref[...]) * (HEAD_DIM ** -0.5)
    fg = lb + (1.0 - lb) * _sigmoid(f_ref[...])
    k = 1.0 - fg
    v = i_ref[...]
    lf = jnp.log(fg)
    row, col = _iota2((C, C))
    gcum = _dot_exact_lhs((row >= col).astype(BF16), lf)

    kpad[sb:sb + C, :] = k
    gpad[sb:sb + C, :] = gcum
    vpad[sb:sb + C, :] = v

    rpos = lax.broadcasted_iota(jnp.int32, (C, HD), 0) % sb
    o_diag = [jnp.zeros((C, HEAD_DIM), F32) for _ in range(N_HEADS)]
    for dl in range(sb):
        kd = kpad[sb - dl:sb - dl + C, :]
        gd = gpad[sb - dl:sb - dl + C, :]
        vd = vpad[sb - dl:sb - dl + C, :]
        ex = jnp.exp(jnp.where(rpos >= dl, gcum - gd, -jnp.inf))
        t = q * kd * ex
        for h in range(N_HEADS):
            hs = slice(h * HEAD_DIM, (h + 1) * HEAD_DIM)
            s = jnp.sum(t[:, hs], axis=-1, keepdims=True)
            o_diag[h] = o_diag[h] + s * vd[:, hs]

    eg = jnp.exp(gcum)
    g_last = gcum[C - 1:C, :]
    k_tail = k * jnp.exp(g_last - gcum)
    ng = ng_ref[...]
    nsb = C // sb
    for h in range(N_HEADS):
        hs = slice(h * HEAD_DIM, (h + 1) * HEAD_DIM)
        qh, kh, vh, gh = q[:, hs], k[:, hs], v[:, hs], gcum[:, hs]
        ST = st_scr[h]
        o = _dot_nt(qh * eg[:, hs], ST) + o_diag[h]
        if nsb > 1:
            parts = [jnp.zeros((sb, HEAD_DIM), F32)]
            for I in range(1, nsb):
                gI = gh[I * sb:(I + 1) * sb, :]
                gn = gI[0:1, :]
                qt = qh[I * sb:(I + 1) * sb, :] * jnp.exp(gI - gn)
                kt = kh[0:I * sb, :] * jnp.exp(gn - gh[0:I * sb, :])
                parts.append(_dot(_dot_nt(qt, kt), vh[0:I * sb, :]))
            o = o + jnp.concatenate(parts, axis=0)
        ST_new = ST * jnp.exp(g_last[:, hs]) + _dot_tn(vh, k_tail[:, hs])
        st_scr[h] = ST_new
        o = o * lax.rsqrt(jnp.mean(o * o, axis=-1, keepdims=True) + EPS) * ng
        o_ref[:, hs] = o * _silu(z_ref[:, hs])

        @pl.when(c == nc - 1)
        def _():
            sout_ref[h] = ST_new.T


def _hgrn2(proj, S0, logits, ng, l, B, T):
    C = CHUNK if T % CHUNK == 0 else T
    nc = T // C
    sb = min(SUB, C)
    depth = logits.shape[0]
    const = lambda b, c: (0, 0)
    return pl.pallas_call(
        functools.partial(_hg_kernel, C=C, nc=nc, layer=l),
        grid=(B, nc),
        in_specs=[
            pl.BlockSpec((C, HD), lambda b, c: (b * nc + c, C_HGQ // HD)),
            pl.BlockSpec((C, HD), lambda b, c: (b * nc + c, C_HGF // HD)),
            pl.BlockSpec((C, HD), lambda b, c: (b * nc + c, C_HGI // HD)),
            pl.BlockSpec((C, HD), lambda b, c: (b * nc + c, C_HGZ // HD)),
            pl.BlockSpec((None, N_HEADS, HEAD_DIM, HEAD_DIM), lambda b, c: (b, 0, 0, 0)),
            pl.BlockSpec((depth, HD), const),
            pl.BlockSpec((1, HEAD_DIM), const),
        ],
        out_specs=[
            pl.BlockSpec((C, HD), lambda b, c: (b * nc + c, 0)),
            pl.BlockSpec((None, N_HEADS, HEAD_DIM, HEAD_DIM), lambda b, c: (b, 0, 0, 0)),
        ],
        out_shape=[
            jax.ShapeDtypeStruct((B * T, HD), F32),
            jax.ShapeDtypeStruct((B, N_HEADS, HEAD_DIM, HEAD_DIM), F32),
        ],
        scratch_shapes=[
            pltpu.VMEM((sb + C, HD), F32),
            pltpu.VMEM((sb + C, HD), F32),
            pltpu.VMEM((sb + C, HD), F32),
            pltpu.VMEM((N_HEADS, HEAD_DIM, HEAD_DIM), F32),
        ],
        compiler_params=_cparams(("parallel", "arbitrary")),
        name="hgrn2",
    )(proj, proj, proj, proj, S0, logits, ng)


def _merge_kernel(x_ref, oa_ref, ob_ref, oc_ref, ga_ref, gb_ref, gc_ref, wa_ref, wb_ref, wc_ref,
                  wo_ref, o_ref):
    mixed = (_sigmoid(ga_ref[...]) * _dot(oa_ref[...], wa_ref[...])
             + _sigmoid(gb_ref[...]) * _dot(ob_ref[...], wb_ref[...])
             + _sigmoid(gc_ref[...]) * _dot(oc_ref[...], wc_ref[...]))
    o_ref[...] = x_ref[...] + _dot(mixed, wo_ref[...])


def _merge(x2d, oa, ob, oc, proj, wa, wb, wc, wo, l, tm):
    M, D = x2d.shape
    row = lambda i: (i, 0)
    wspec = lambda shape: pl.BlockSpec((None,) + shape, lambda i: (l, 0, 0))
    g0 = C_GATE // D
    return pl.pallas_call(
        _merge_kernel,
        grid=(M // tm,),
        in_specs=[
            pl.BlockSpec((tm, D), row),
            pl.BlockSpec((tm, HD), row), pl.BlockSpec((tm, HD), row), pl.BlockSpec((tm, HD), row),
            pl.BlockSpec((tm, D), lambda i: (i, g0)),
            pl.BlockSpec((tm, D), lambda i: (i, g0 + 1)),
            pl.BlockSpec((tm, D), lambda i: (i, g0 + 2)),
            wspec((HD, D)), wspec((HD, D)), wspec((HD, D)), wspec((D, D)),
        ],
        out_specs=pl.BlockSpec((tm, D), row),
        out_shape=jax.ShapeDtypeStruct((M, D), F32),
        compiler_params=_cparams(("parallel",)),
        name="merge",
    )(x2d, oa, ob, oc, proj, proj, proj, wa, wb, wc, wo)


def _ffn_kernel(x_ref, g_ref, win_ref, wout_ref, gf_ref, o_ref, *, dff, tf, final):
    x = x_ref[...]
    h = (x * lax.rsqrt(jnp.mean(x * x, axis=-1, keepdims=True) + EPS) * g_ref[...]).astype(BF16)
    acc = x
    for j in range(dff // tf):
        gg = jnp.dot(h, win_ref[:, j * tf:(j + 1) * tf], preferred_element_type=F32)
        uu = jnp.dot(h, win_ref[:, dff + j * tf:dff + (j + 1) * tf], preferred_element_type=F32)
        act = (_silu(gg) * uu).astype(BF16)
        acc = acc + jnp.dot(act, wout_ref[j * tf:(j + 1) * tf, :], preferred_element_type=F32)
    if final:
        acc = acc * lax.rsqrt(jnp.mean(acc * acc, axis=-1, keepdims=True) + EPS) * gf_ref[...]
    o_ref[...] = acc


def _ffn(x2d, g_all, win, wout, gf, l, tm, final):
    M, D = x2d.shape
    dff = wout.shape[1]
    tf = 256
    return pl.pallas_call(
        functools.partial(_ffn_kernel, dff=dff, tf=tf, final=final),
        grid=(M // tm,),
        in_specs=[
            pl.BlockSpec((tm, D), lambda i: (i, 0)),
            pl.BlockSpec((None, 1, D), lambda i: (l, 0, 0)),
            pl.BlockSpec((None, D, 2 * dff), lambda i: (l, 0, 0)),
            pl.BlockSpec((None, dff, D), lambda i: (l, 0, 0)),
            pl.BlockSpec((1, D), lambda i: (0, 0)),
        ],
        out_specs=pl.BlockSpec((tm, D), lambda i: (i, 0)),
        out_shape=jax.ShapeDtypeStruct((M, D), F32),
        compiler_params=_cparams(("parallel",)),
        name="ffn",
    )(x2d, g_all, win, wout, gf)


def _row_tile(M, cap):
    t = min(cap, M)
    while M % t:
        t //= 2
    return t


def _prep_weights(w_in, w_br_a, w_br_b, w_br_c, w_o, w_ffn_in, w_ffn_out):
    depth, D, _ = w_in.shape
    o_dnqkv = 3 * HD
    o_a = o_dnqkv + 3 * HD
    o_z = o_a + 2 * N_HEADS
    o_hg = o_z + HD
    o_gate = o_hg + 4 * HD
    pad = N_PROJ - (C_AB + 2 * N_HEADS)
    w_perm = jnp.concatenate([
        w_in[:, :, 0:o_a],
        w_in[:, :, o_gate:o_gate + 3 * D],
        w_in[:, :, o_z:o_hg],
        w_in[:, :, o_hg:o_gate],
        w_in[:, :, o_a:o_z],
        jnp.zeros((depth, D, pad), w_in.dtype),
    ], axis=-1).astype(BF16)
    bf = lambda a: a.astype(BF16)
    return w_perm, bf(w_br_a), bf(w_br_b), bf(w_br_c), bf(w_o), bf(w_ffn_in), bf(w_ffn_out)


def _pad_lanes(v, n=LANE):
    return jnp.pad(v, (0, n - v.shape[0]))


def _trunk(x, P, page_table, cache_k, cache_v, dn_conv, dn_S, hg_S):
    B, T, D = x.shape
    depth = P['ln1_g'].shape[0]
    M = B * T
    x2d = x.reshape(M, D)
    C = CHUNK if T % CHUNK == 0 else T
    nc = T // C
    tm_proj = _row_tile(M, 1024)
    tm_tok = _row_tile(M, 512)
    W3 = 3 * HD
    ks, vs, bufs, dSs, hSs = [], [], [], [], []
    for l in range(depth):
        proj = _inproj(x2d, P['ln1_g3'], P['w_perm'], l, tm_proj)
        k_new = proj[:, C_SBK:C_SBK + HD].reshape(B, T, N_HEADS, HEAD_DIM)
        v_new = proj[:, C_SBV:C_SBV + HD].reshape(B, T, N_HEADS, HEAD_DIM)
        qkv3 = proj[:, C_DNQKV:C_DNQKV + W3].reshape(B, T, W3)
        if page_table is None:
            o_a = _sb_prompt(proj, P['sb_bias'][l], B, T)
            conv0 = jnp.zeros((B, 8, W3), F32)
            Sb0 = jnp.zeros((B, N_HEADS, HEAD_DIM, HEAD_DIM), F32)
            Sc0 = Sb0
            buf_new = qkv3[:, T - (CONV_W - 1):, :]
        else:
            o_a = _sb_decode(proj, cache_k, cache_v, page_table, P['sb_bias'][l], l, B, T)
            conv0 = jnp.pad(dn_conv[l], ((0, 0), (8 - (CONV_W - 1), 0), (0, 0)))
            Sb0, Sc0 = dn_S[l], hg_S[l]
            buf_new = jnp.concatenate([dn_conv[l], qkv3], axis=1)[:, T:, :]
        abT = proj[:, C_AB:C_AB + 8].reshape(B, nc, C, 8).transpose(0, 1, 3, 2)
        alog = P['dn_A_log'][l]
        dtb = P['dn_dt_bias'][l]
        alog_row = _pad_lanes(alog)[None, :]
        dtb_row = _pad_lanes(dtb)[None, :]
        alog_col = jnp.broadcast_to(_pad_lanes(alog, 8)[:, None], (8, LANE))
        dtb_col = jnp.broadcast_to(_pad_lanes(dtb, 8)[:, None], (8, LANE))
        o_b, Sb = _deltanet(proj, abT, conv0, Sb0, P['dn_conv_w'][l], alog_row, dtb_row, alog_col,
                            dtb_col, P['dn_norm_g'][l][None, :], B, T)
        o_c, Sc = _hgrn2(proj, Sc0, P['hg_lb_logits'], P['hg_norm_g'][l][None, :], l, B, T)
        x2d = _merge(x2d, o_a, o_b, o_c, proj, P['w_br_a'], P['w_br_b'], P['w_br_c'], P['w_o'], l, tm_tok)
        x2d = _ffn(x2d, P['ln2_g3'], P['w_ffn_in'], P['w_ffn_out'], P['lnf_g'][None, :], l, tm_tok,
                   final=(l == depth - 1))
        ks.append(k_new)
        vs.append(v_new)
        bufs.append(buf_new)
        dSs.append(Sb)
        hSs.append(Sc)
    y = x2d.reshape(B, T, D)
    return y, jnp.stack(ks), jnp.stack(vs), jnp.stack(bufs), jnp.stack(dSs), jnp.stack(hSs)


def kernel(x_prompt, x_sample, cache_sb_k, cache_sb_v, state_dn_conv, state_dn_S, state_hg_S, page_table,
           ln1_g, w_in, sb_bias, dn_conv_w, dn_A_log, dn_dt_bias, dn_norm_g, hg_lb_logits, hg_norm_g,
           w_br_a, w_br_b, w_br_c, w_o, ln2_g, w_ffn_in, w_ffn_out, lnf_g):
    w_perm, wa, wb, wc, wo, wfi, wfo = _prep_weights(w_in, w_br_a, w_br_b, w_br_c, w_o, w_ffn_in, w_ffn_out)
    P = dict(ln1_g3=ln1_g[:, None, :], ln2_g3=ln2_g[:, None, :], w_perm=w_perm, sb_bias=sb_bias,
             dn_conv_w=dn_conv_w, dn_A_log=dn_A_log, dn_dt_bias=dn_dt_bias, dn_norm_g=dn_norm_g,
             hg_lb_logits=hg_lb_logits.astype(F32), hg_norm_g=hg_norm_g, w_br_a=wa, w_br_b=wb, w_br_c=wc,
             w_o=wo, w_ffn_in=wfi, w_ffn_out=wfo, lnf_g=lnf_g, ln1_g=ln1_g)
    depth, n_pool = cache_sb_k.shape[:2]
    ck = cache_sb_k.reshape(depth, n_pool, PAGE, HD)
    cv = cache_sb_v.reshape(depth, n_pool, PAGE, HD)
    y_p, sbk_p, sbv_p, conv_p, dnS_p, hgS_p = _trunk(x_prompt, P, None, None, None, None, None, None)
    y_s, sbk_s, sbv_s, conv_s, dnS_s, hgS_s = _trunk(x_sample, P, page_table, ck, cv, state_dn_conv,
                                                     state_dn_S, state_hg_S)
    return (y_p, y_s, sbk_p, sbv_p, sbk_s, sbv_s, conv_p, conv_s, dnS_p, dnS_s, hgS_p, hgS_s)
```

```python
import functools
import math

import jax
import jax.numpy as jnp
from jax import lax
from jax.experimental import pallas as pl
from jax.experimental.pallas import tpu as pltpu

F32 = jnp.float32
BF16 = jnp.bfloat16
EPS = 1e-6
LOG2E = math.log2(math.e)

N_HEADS = 4
HEAD_DIM = 128
HD = N_HEADS * HEAD_DIM
CONV_W = 4
PAGE = 128
PAGE_ROWS = PAGE * N_HEADS
CHUNK = 64
SUB = 16
LANE = 128
DN_ROWS = 128
VMEM_LIMIT = 56 * 1024 * 1024

C_SBQ, C_SBK, C_SBV = 0, 512, 1024
C_DNQKV = 1536
C_GATE = 3072
C_DNZ = 6144
C_HGQ, C_HGF, C_HGI, C_HGZ = 6656, 7168, 7680, 8192
C_AB = 8704
N_PROJ = 8960
PROJ_TN = 1280


def _sigmoid(x):
    return 1.0 / (1.0 + jnp.exp(-x))


def _silu(x):
    return x * _sigmoid(x)


def _softplus(x):
    return jnp.maximum(x, 0.0) + jnp.log(1.0 + jnp.exp(-jnp.abs(x)))


def _dot(a, b):
    return jnp.dot(a.astype(BF16), b.astype(BF16), preferred_element_type=F32)


def _dot_nt(a, b):
    return lax.dot_general(a.astype(BF16), b.astype(BF16), (((1,), (1,)), ((), ())),
                           preferred_element_type=F32)


def _dot_tn(a, b):
    return lax.dot_general(a.astype(BF16), b.astype(BF16), (((0,), (0,)), ((), ())),
                           preferred_element_type=F32)


def _split(a, n):
    out = []
    r = a
    for t in range(n):
        p = r.astype(BF16)
        out.append(p)
        if t + 1 < n:
            r = r - p.astype(F32)
    return out


def _dot_exact_rhs(a, m_bf16, n=3):
    acc = None
    for p in _split(a, n):
        t = jnp.dot(p, m_bf16, preferred_element_type=F32)
        acc = t if acc is None else acc + t
    return acc


def _dot_exact_lhs(m_bf16, a, n=3):
    acc = None
    for p in _split(a, n):
        t = jnp.dot(m_bf16, p, preferred_element_type=F32)
        acc = t if acc is None else acc + t
    return acc


def _dot_hp(a, b, nt=False):
    a1, a2 = _split(a, 2)
    b1, b2 = _split(b, 2)
    if nt:
        f = lambda x, y: lax.dot_general(x, y, (((1,), (1,)), ((), ())), preferred_element_type=F32)
    else:
        f = lambda x, y: jnp.dot(x, y, preferred_element_type=F32)
    return f(a1, b1) + (f(a1, b2) + f(a2, b1))


def _iota2(shape):
    return (lax.broadcasted_iota(jnp.int32, shape, 0), lax.broadcasted_iota(jnp.int32, shape, 1))


def _log2(n):
    k = n.bit_length() - 1
    assert (1 << k) == n
    return k


def _cparams(sem):
    return pltpu.CompilerParams(dimension_semantics=sem, vmem_limit_bytes=VMEM_LIMIT)


def _inproj_kernel(x_ref, g_ref, w_ref, o_ref, h_scr):
    @pl.when(pl.program_id(1) == 0)
    def _():
        x = x_ref[...]
        ms = jnp.mean(x * x, axis=-1, keepdims=True)
        h_scr[...] = (x * lax.rsqrt(ms + EPS) * g_ref[...]).astype(BF16)

    o_ref[...] = jnp.dot(h_scr[...], w_ref[...], preferred_element_type=F32)


def _inproj(x2d, g_all, w_all, l, tm):
    M, D = x2d.shape
    return pl.pallas_call(
        _inproj_kernel,
        grid=(M // tm, N_PROJ // PROJ_TN),
        in_specs=[
            pl.BlockSpec((tm, D), lambda i, j: (i, 0)),
            pl.BlockSpec((None, 1, D), lambda i, j: (l, 0, 0)),
            pl.BlockSpec((None, D, PROJ_TN), lambda i, j: (l, 0, j)),
        ],
        out_specs=pl.BlockSpec((tm, PROJ_TN), lambda i, j: (i, j)),
        out_shape=jax.ShapeDtypeStruct((M, N_PROJ), F32),
        scratch_shapes=[pltpu.VMEM((tm, D), BF16)],
        compiler_params=_cparams(("parallel", "arbitrary")),
        name="inproj",
    )(x2d, g_all, w_all)


def _log2_sig_pair(z2):
    lg = jnp.log2(1.0 + jnp.exp2(-jnp.abs(z2)))
    ls = jnp.minimum(z2, 0.0) - lg
    return ls, ls - z2


def _sbp_block(q, k, v, bias2, U, acc_ref, csum_ref, valid):
    z2 = _dot_nt(q, k) + bias2
    ls, l1m = _log2_sig_pair(z2)
    if valid is not None:
        l1m = jnp.where(valid, l1m, 0.0)
    hi, lo = _split(l1m, 2)
    suffix = jnp.dot(hi, U, preferred_element_type=F32) + jnp.dot(lo, U, preferred_element_type=F32)
    w = jnp.exp2(ls + suffix + csum_ref[...])
    if valid is not None:
        w = jnp.where(valid, w, 0.0)
    acc_ref[...] += _dot(w, v)
    csum_ref[...] += jnp.sum(l1m, axis=-1, keepdims=True)


def _sbp_kernel(bias_ref, q_ref, k_ref, v_ref, u_ref, o_ref, kbf, vbf, acc, csum, *, tq):
    h = pl.program_id(1)
    i = pl.program_id(2)

    @pl.when(i == 0)
    def _():
        kbf[...] = k_ref[...].astype(BF16)
        vbf[...] = v_ref[...].astype(BF16)

    bias2 = bias_ref[h] * LOG2E
    q = (q_ref[...] * (HEAD_DIM ** -0.5 * LOG2E)).astype(BF16)
    U = u_ref[...]
    acc[...] = jnp.zeros_like(acc)
    csum[...] = jnp.zeros_like(csum)

    row, col = _iota2((tq, tq))
    start = pl.multiple_of(i * tq, tq)
    _sbp_block(q, kbf[pl.ds(start, tq), :], vbf[pl.ds(start, tq), :], bias2, U, acc, csum, col < row)

    def body(jj, carry):
        s = pl.multiple_of((i - 1 - jj) * tq, tq)
        _sbp_block(q, kbf[pl.ds(s, tq), :], vbf[pl.ds(s, tq), :], bias2, U, acc, csum, None)
        return carry

    lax.fori_loop(0, i, body, 0)
    o_ref[...] = acc[...]


def _sb_prompt(proj, bias_l, B, T):
    tq = min(256, T)
    nq = T // tq
    r = jnp.arange(tq)
    U = (r[:, None] > r[None, :]).astype(BF16)
    return pl.pallas_call(
        functools.partial(_sbp_kernel, tq=tq),
        grid=(B, N_HEADS, nq),
        in_specs=[
            pl.BlockSpec(memory_space=pltpu.SMEM),
            pl.BlockSpec((tq, HEAD_DIM), lambda b, h, i: (b * nq + i, C_SBQ // HEAD_DIM + h)),
            pl.BlockSpec((T, HEAD_DIM), lambda b, h, i: (b, C_SBK // HEAD_DIM + h)),
            pl.BlockSpec((T, HEAD_DIM), lambda b, h, i: (b, C_SBV // HEAD_DIM + h)),
            pl.BlockSpec((tq, tq), lambda b, h, i: (0, 0)),
        ],
        out_specs=pl.BlockSpec((tq, HEAD_DIM), lambda b, h, i: (b * nq + i, h)),
        out_shape=jax.ShapeDtypeStruct((B * T, HD), F32),
        scratch_shapes=[
            pltpu.VMEM((T, HEAD_DIM), BF16),
            pltpu.VMEM((T, HEAD_DIM), BF16),
            pltpu.VMEM((tq, HEAD_DIM), F32),
            pltpu.VMEM((tq, 1), F32),
        ],
        compiler_params=_cparams(("parallel", "parallel", "arbitrary")),
        name="sb_prompt",
    )(bias_l, proj, proj, proj, U)


def _sbd_group(qr, ks, vs, bias2, U, hm, extra_valid, acc_ref, csum_ref):
    n = len(ks)
    kall = ks[0] if n == 1 else jnp.concatenate(ks, axis=0)
    vall = vs[0] if n == 1 else jnp.concatenate(vs, axis=0)
    z2 = _dot_nt(qr, kall) + bias2
    ls, l1m = _log2_sig_pair(z2)
    valid = hm if extra_valid is None else extra_valid
    valid_all = valid if n == 1 else jnp.concatenate([valid] * n, axis=1)
    l1m = jnp.where(valid_all, l1m, 0.0)
    pages = [l1m[:, p * PAGE_ROWS:(p + 1) * PAGE_ROWS] for p in range(n)]
    lstack = pages[0] if n == 1 else jnp.concatenate(pages, axis=0)
    hi, lo = _split(lstack, 2)
    suf = jnp.dot(hi, U, preferred_element_type=F32) + jnp.dot(lo, U, preferred_element_type=F32)
    rows = qr.shape[0]
    c = csum_ref[...]
    ws = []
    for p in range(n):
        e = ls[:, p * PAGE_ROWS:(p + 1) * PAGE_ROWS] + suf[p * rows:(p + 1) * rows, :] + c
        ws.append(jnp.where(valid, jnp.exp2(e), 0.0).astype(BF16))
        c = c + jnp.sum(pages[p], axis=-1, keepdims=True)
    wall = ws[0] if n == 1 else jnp.concatenate(ws, axis=1)
    acc_ref[...] += jnp.dot(wall, vall, preferred_element_type=F32)
    csum_ref[...] = c


def _sbd_kernel(pt_ref, bias_ref, q_ref, kn_ref, vn_ref, *rest, G, n_groups, ts):
    kpages = rest[:G]
    vpages = rest[G:2 * G]
    u_ref, o_ref, qr, kbuf, vbuf, acc, csum = rest[2 * G:]
    del pt_ref
    g = pl.program_id(1)
    rows = N_HEADS * ts
    U = u_ref[...]
    bias2 = bias_ref[:, 0:1] * LOG2E
    row, col = _iota2((rows, PAGE_ROWS))
    hm = (col & (N_HEADS - 1)) == (row // ts)

    @pl.when(g == 0)
    def _():
        acc[...] = jnp.zeros_like(acc)
        csum[...] = jnp.zeros_like(csum)
        q = q_ref[...] * (HEAD_DIM ** -0.5 * LOG2E)
        for hh in range(N_HEADS):
            qr[hh * ts:(hh + 1) * ts, :] = q[:, hh * HEAD_DIM:(hh + 1) * HEAD_DIM].astype(BF16)
        kbuf[...] = jnp.zeros_like(kbuf)
        vbuf[...] = jnp.zeros_like(vbuf)
        kbuf[0:ts * N_HEADS, :] = kn_ref[...].astype(BF16)
        vbuf[0:ts * N_HEADS, :] = vn_ref[...].astype(BF16)
        valid_new = hm & ((col >> 2) < (row % ts))
        _sbd_group(qr[...], [kbuf[...]], [vbuf[...]], bias2, U, hm, valid_new, acc, csum)

    @pl.when(g > 0)
    def _():
        _sbd_group(qr[...], [r[...].astype(BF16) for r in kpages], [r[...].astype(BF16) for r in vpages],
                   bias2, U, hm, None, acc, csum)

    @pl.when(g == n_groups)
    def _():
        for hh in range(N_HEADS):
            o_ref[:, hh * HEAD_DIM:(hh + 1) * HEAD_DIM] = acc[hh * ts:(hh + 1) * ts, :]


def _sb_decode(proj, cache_k, cache_v, page_table, bias_l, l, nb, ts):
    npg = page_table.shape[1]
    G = min(8, npg)
    n_groups = npg // G
    rows = N_HEADS * ts
    r = jnp.arange(PAGE_ROWS)
    U = (((r[:, None] & 3) == (r[None, :] & 3)) & (r[:, None] > r[None, :])).astype(BF16)
    bias_rows = jnp.broadcast_to(jnp.repeat(bias_l, ts)[:, None], (rows, LANE)).astype(F32)
    kn = proj[:, C_SBK:C_SBK + HD].reshape(nb * ts * N_HEADS, HEAD_DIM)
    vn = proj[:, C_SBV:C_SBV + HD].reshape(nb * ts * N_HEADS, HEAD_DIM)

    def page_spec(p):
        def imap(b, g, pt):
            first = jnp.maximum(g - 1, 0) * G
            return (l, pt[b, npg - 1 - (first + p)], 0, 0)
        return pl.BlockSpec((None, None, PAGE_ROWS, HEAD_DIM), imap)

    in_specs = [
        pl.BlockSpec((rows, LANE), lambda b, g, pt: (0, 0)),
        pl.BlockSpec((ts, HD), lambda b, g, pt: (b, C_SBQ // HD)),
        pl.BlockSpec((rows, HEAD_DIM), lambda b, g, pt: (b, 0)),
        pl.BlockSpec((rows, HEAD_DIM), lambda b, g, pt: (b, 0)),
    ] + [page_spec(p) for p in range(G)] + [page_spec(p) for p in range(G)] + [
        pl.BlockSpec((PAGE_ROWS, PAGE_ROWS), lambda b, g, pt: (0, 0)),
    ]
    grid_spec = pltpu.PrefetchScalarGridSpec(
        num_scalar_prefetch=1,
        grid=(nb, n_groups + 1),
        in_specs=in_specs,
        out_specs=pl.BlockSpec((ts, HD), lambda b, g, pt: (b, 0)),
        scratch_shapes=[
            pltpu.VMEM((rows, HEAD_DIM), BF16),
            pltpu.VMEM((PAGE_ROWS, HEAD_DIM), BF16),
            pltpu.VMEM((PAGE_ROWS, HEAD_DIM), BF16),
            pltpu.VMEM((rows, HEAD_DIM), F32),
            pltpu.VMEM((rows, 1), F32),
        ],
    )
    return pl.pallas_call(
        functools.partial(_sbd_kernel, G=G, n_groups=n_groups, ts=ts),
        grid_spec=grid_spec,
        out_shape=jax.ShapeDtypeStruct((nb * ts, HD), F32),
        compiler_params=_cparams(("parallel", "arbitrary")),
        name="sb_decode",
    )(page_table, bias_rows, proj, kn, vn, *([cache_k] * G), *([cache_v] * G), U)


def _unit_lower_inverse(As, C, row, col):
    eye = (row == col).astype(F32)
    blk = min(SUB, C)
    sh = _log2(blk)
    same = (row >> sh) == (col >> sh)
    Ms = [-jnp.where(same, A, 0.0) for A in As]
    Rs = [eye + M for M in Ms]
    for _ in range(sh - 1):
        Ms = [_dot_hp(M, M) for M in Ms]
        Rs = [R + _dot_hp(R, M) for R, M in zip(Rs, Ms)]
    if C > blk:
        Ps = [-_dot_hp(R, jnp.where(same, 0.0, A)) for R, A in zip(Rs, As)]
        Xs = [eye + P for P in Ps]
        for _ in range(_log2(C // blk) - 1):
            Ps = [_dot_hp(P, P) for P in Ps]
            Xs = [X + _dot_hp(X, P) for X, P in zip(Xs, Ps)]
        Rs = [_dot_hp(X, R) for X, R in zip(Xs, Rs)]
    return Rs


def _dn_kernel(qkv_ref, z_ref, ab_ref, abT_ref, conv0_ref, s0_ref, convw_ref, alog_row_ref,
               dtb_row_ref, alog_col_ref, dtb_col_ref, ng_ref, o_ref, sout_ref, xbuf, s_scr,
               *, ns, rs, C, nsteps):
    c = pl.program_id(1)
    R = ns * rs
    npc = rs // C

    @pl.when(c == 0)
    def _():
        xbuf[:, 0:8, :] = conv0_ref[...]
        s_scr[...] = s0_ref[...]

    @pl.when(c > 0)
    def _():
        xbuf[:, 0:8, :] = xbuf[:, rs:rs + 8, :]

    ys = []
    for j in range(ns):
        xbuf[j, 8:8 + rs, :] = qkv_ref[j * rs:(j + 1) * rs, :]
        acc = xbuf[j, 5:5 + rs, :] * convw_ref[0:1, :]
        for t in range(1, CONV_W):
            acc = acc + xbuf[j, 5 + t:5 + t + rs, :] * convw_ref[t:t + 1, :]
        ys.append(acc)
    y = _silu(ys[0] if ns == 1 else jnp.concatenate(ys, axis=0))

    row, col = _iota2((R, R))
    shc = _log2(C)
    same_chunk = (row >> shc) == (col >> shc)
    lower_incl = same_chunk & (row >= col)
    tri_lower = lower_incl.astype(BF16)
    tri_upper = (same_chunk & (row <= col)).astype(BF16)

    ab = ab_ref[...]
    g_col = -jnp.exp(alog_row_ref[...]) * _softplus(ab + dtb_row_ref[...])
    beta_all = _sigmoid(ab)
    gcum_col = _dot_exact_lhs(tri_lower, g_col)
    abT = abT_ref[...]
    g_row = -jnp.exp(alog_col_ref[...]) * _softplus(abT + dtb_col_ref[...])
    gcum_row = _dot_exact_rhs(g_row, tri_upper)

    ng = ng_ref[...]
    heads = range(N_HEADS)
    qs, kls, vs, betas, gcs, decays, kbs = [], [], [], [], [], [], []
    for h in heads:
        q = y[:, h * HEAD_DIM:(h + 1) * HEAD_DIM]
        k = y[:, HD + h * HEAD_DIM:HD + (h + 1) * HEAD_DIM]
        vs.append(y[:, 2 * HD + h * HEAD_DIM:2 * HD + (h + 1) * HEAD_DIM])
        qs.append(q * lax.rsqrt(jnp.sum(q * q, axis=-1, keepdims=True) + EPS) * (HEAD_DIM ** -0.5))
        kls.append(k * lax.rsqrt(jnp.sum(k * k, axis=-1, keepdims=True) + EPS))
        betas.append(beta_all[:, N_HEADS + h:N_HEADS + h + 1])
        gcs.append(gcum_col[:, h:h + 1])
        gr = gcum_row[h:h + 1, :]
        decays.append(jnp.exp(jnp.where(lower_incl, gcs[h] - gr, -jnp.inf)))
        kbs.append(kls[h] * betas[h])
    As = [jnp.where(row > col, _dot_hp(kbs[h], kls[h], nt=True) * decays[h], 0.0) for h in heads]
    Tms = _unit_lower_inverse(As, C, row, col)
    egs = [jnp.exp(gcs[h]) for h in heads]
    sols = [_dot_hp(Tms[h], jnp.concatenate([vs[h] * betas[h], kbs[h] * egs[h]], axis=-1)) for h in heads]
    us = [s[:, :HEAD_DIM] for s in sols]
    ws = [s[:, HEAD_DIM:] for s in sols]
    attns = [(_dot_nt(qs[h], kls[h]) * decays[h]).astype(BF16) for h in heads]
    qgs = [qs[h] * egs[h] for h in heads]
    v_news = [[] for _ in heads]
    o_inters = [[] for _ in heads]
    for j in range(ns):
        Ss = [s_scr[j, h] for h in heads]
        for i in range(npc):
            r0 = j * rs + i * C
            rows = slice(r0, r0 + C)
            for h in heads:
                v_new = us[h][rows, :] - _dot(ws[h][rows, :], Ss[h])
                o_inters[h].append(_dot(qgs[h][rows, :], Ss[h]))
                g_last = gcs[h][r0 + C - 1:r0 + C, :]
                kg = kls[h][rows, :] * jnp.exp(g_last - gcs[h][rows, :])
                Ss[h] = Ss[h] * jnp.exp(g_last) + _dot_tn(kg, v_new)
                v_news[h].append(v_new)
        for h in heads:
            s_scr[j, h] = Ss[h]
    for h in heads:
        hs = slice(h * HEAD_DIM, (h + 1) * HEAD_DIM)
        v_new_all = v_news[h][0] if len(v_news[h]) == 1 else jnp.concatenate(v_news[h], axis=0)
        o_inter = o_inters[h][0] if len(o_inters[h]) == 1 else jnp.concatenate(o_inters[h], axis=0)
        o = o_inter + _dot(attns[h], v_new_all)
        o = o * lax.rsqrt(jnp.mean(o * o, axis=-1, keepdims=True) + EPS) * ng
        o_ref[:, hs] = o * _silu(z_ref[:, hs])

    @pl.when(c == nsteps - 1)
    def _():
        sout_ref[...] = s_scr[...]


def _deltanet(proj, abT, conv0, S0, convw_l, alog_row, dtb_row, alog_col, dtb_col, ng, B, T):
    C = CHUNK if T % CHUNK == 0 else T
    if T >= DN_ROWS:
        ns, rs = 1, DN_ROWS
    else:
        ns, rs = min(B, DN_ROWS // T), T
    assert T % rs == 0 and rs % C == 0 and B % ns == 0
    R = ns * rs
    nsteps = T // rs
    W3 = 3 * HD
    const = lambda b, c: (0, 0)
    return pl.pallas_call(
        functools.partial(_dn_kernel, ns=ns, rs=rs, C=C, nsteps=nsteps),
        grid=(B // ns, nsteps),
        in_specs=[
            pl.BlockSpec((R, W3), lambda b, c: (b * nsteps + c, C_DNQKV // W3)),
            pl.BlockSpec((R, HD), lambda b, c: (b * nsteps + c, C_DNZ // HD)),
            pl.BlockSpec((R, LANE), lambda b, c: (b * nsteps + c, C_AB // LANE)),
            pl.BlockSpec((8, R), lambda b, c: (0, b * nsteps + c)),
            pl.BlockSpec((ns, 8, W3), lambda b, c: (b, 0, 0)),
            pl.BlockSpec((ns, N_HEADS, HEAD_DIM, HEAD_DIM), lambda b, c: (b, 0, 0, 0)),
            pl.BlockSpec((CONV_W, W3), const),
            pl.BlockSpec((1, LANE), const),
            pl.BlockSpec((1, LANE), const),
            pl.BlockSpec((8, R), const),
            pl.BlockSpec((8, R), const),
            pl.BlockSpec((1, HEAD_DIM), const),
        ],
        out_specs=[
            pl.BlockSpec((R, HD), lambda b, c: (b * nsteps + c, 0)),
            pl.BlockSpec((ns, N_HEADS, HEAD_DIM, HEAD_DIM), lambda b, c: (b, 0, 0, 0)),
        ],
        out_shape=[
            jax.ShapeDtypeStruct((B * T, HD), F32),
            jax.ShapeDtypeStruct((B, N_HEADS, HEAD_DIM, HEAD_DIM), F32),
        ],
        scratch_shapes=[
            pltpu.VMEM((ns, 8 + rs, W3), F32),
            pltpu.VMEM((ns, N_HEADS, HEAD_DIM, HEAD_DIM), F32),
        ],
        compiler_params=_cparams(("parallel", "arbitrary")),
        name="deltanet",
    )(proj, proj, proj, abT, conv0, S0, convw_l, alog_row, dtb_row, alog_col, dtb_col, ng)


def _hg_kernel(q_ref, f_ref, i_ref, z_ref, s0_ref, logit_ref, ng_ref, o_ref, sout_ref,
               kpad, gpad, vpad, st_scr, *, C, nc, layer):
    c = pl.program_id(1)
    sb = min(SUB, C)

    @pl.when(c == 0)
    def _():
        for h in range(N_HEADS):
            st_scr[h] = s0_ref[h].T
        kpad[0:sb, :] = jnp.zeros((sb, HD), F32)
        gpad[0:sb, :] = jnp.zeros((sb, HD), F32)
        vpad[0:sb, :] = jnp.zeros((sb, HD), F32)

    logits = logit_ref[...]
    e = jnp.exp(logits - jnp.max(logits, axis=0, keepdims=True))
    p = e / jnp.sum(e, axis=0, keepdims=True)
    lb = jnp.sum(p[0:layer + 1, :], axis=0, keepdims=True) - p[0:1, :]

    q = _silu(q_ref[...]) * (HEAD_DIM ** -0.5)
    fg = lb + (1.0 - lb) * _sigmoid(f_ref[...])
    k = 1.0 - fg
    v = i_ref[...]
    lf = jnp.log(fg)
    row, col = _iota2((C, C))
    gcum = _dot_exact_lhs((row >= col).astype(BF16), lf)

    kpad[sb:sb + C, :] = k
    gpad[sb:sb + C, :] = gcum
    vpad[sb:sb + C, :] = v

    rpos = lax.broadcasted_iota(jnp.int32, (C, HD), 0) % sb
    o_diag = [jnp.zeros((C, HEAD_DIM), F32) for _ in range(N_HEADS)]
    for dl in range(sb):
        kd = kpad[sb - dl:sb - dl + C, :]
        gd = gpad[sb - dl:sb - dl + C, :]
        vd = vpad[sb - dl:sb - dl + C, :]
        ex = jnp.exp(jnp.where(rpos >= dl, gcum - gd, -jnp.inf))
        t = q * kd * ex
        for h in range(N_HEADS):
            hs = slice(h * HEAD_DIM, (h + 1) * HEAD_DIM)
            s = jnp.sum(t[:, hs], axis=-1, keepdims=True)
            o_diag[h] = o_diag[h] + s * vd[:, hs]

    eg = jnp.exp(gcum)
    g_last = gcum[C - 1:C, :]
    k_tail = k * jnp.exp(g_last - gcum)
    ng = ng_ref[...]
    nsb = C // sb
    for h in range(N_HEADS):
        hs = slice(h * HEAD_DIM, (h + 1) * HEAD_DIM)
        qh, kh, vh, gh = q[:, hs], k[:, hs], v[:, hs], gcum[:, hs]
        ST = st_scr[h]
        o = _dot_nt(qh * eg[:, hs], ST) + o_diag[h]
        if nsb > 1:
            parts = [jnp.zeros((sb, HEAD_DIM), F32)]
            for I in range(1, nsb):
                gI = gh[I * sb:(I + 1) * sb, :]
                gn = gI[0:1, :]
                qt = qh[I * sb:(I + 1) * sb, :] * jnp.exp(gI - gn)
                kt = kh[0:I * sb, :] * jnp.exp(gn - gh[0:I * sb, :])
                parts.append(_dot(_dot_nt(qt, kt), vh[0:I * sb, :]))
            o = o + jnp.concatenate(parts, axis=0)
        st_scr[h] = ST * jnp.exp(g_last[:, hs]) + _dot_tn(vh, k_tail[:, hs])
        o = o * lax.rsqrt(jnp.mean(o * o, axis=-1, keepdims=True) + EPS) * ng
        o_ref[:, hs] = o * _silu(z_ref[:, hs])

    @pl.when(c == nc - 1)
    def _():
        for h in range(N_HEADS):
            sout_ref[h] = st_scr[h].T


def _hgrn2(proj, S0, logits, ng, l, B, T):
    C = CHUNK if T % CHUNK == 0 else T
    nc = T // C
    sb = min(SUB, C)
    depth = logits.shape[0]
    const = lambda b, c: (0, 0)
    return pl.pallas_call(
        functools.partial(_hg_kernel, C=C, nc=nc, layer=l),
        grid=(B, nc),
        in_specs=[
            pl.BlockSpec((C, HD), lambda b, c: (b * nc + c, C_HGQ // HD)),
            pl.BlockSpec((C, HD), lambda b, c: (b * nc + c, C_HGF // HD)),
            pl.BlockSpec((C, HD), lambda b, c: (b * nc + c, C_HGI // HD)),
            pl.BlockSpec((C, HD), lambda b, c: (b * nc + c, C_HGZ // HD)),
            pl.BlockSpec((None, N_HEADS, HEAD_DIM, HEAD_DIM), lambda b, c: (b, 0, 0, 0)),
            pl.BlockSpec((depth, HD), const),
            pl.BlockSpec((1, HEAD_DIM), const),
        ],
        out_specs=[
            pl.BlockSpec((C, HD), lambda b, c: (b * nc + c, 0)),
            pl.BlockSpec((None, N_HEADS, HEAD_DIM, HEAD_DIM), lambda b, c: (b, 0, 0, 0)),
        ],
        out_shape=[
            jax.ShapeDtypeStruct((B * T, HD), F32),
            jax.ShapeDtypeStruct((B, N_HEADS, HEAD_DIM, HEAD_DIM), F32),
        ],
        scratch_shapes=[
            pltpu.VMEM((sb + C, HD), F32),
            pltpu.VMEM((sb + C, HD), F32),
            pltpu.VMEM((sb + C, HD), F32),
            pltpu.VMEM((N_HEADS, HEAD_DIM, HEAD_DIM), F32),
        ],
        compiler_params=_cparams(("parallel", "arbitrary")),
        name="hgrn2",
    )(proj, proj, proj, proj, S0, logits, ng)


def _merge_kernel(x_ref, oa_ref, ob_ref, oc_ref, ga_ref, gb_ref, gc_ref, wa_ref, wb_ref, wc_ref,
                  wo_ref, o_ref):
    mixed = (_sigmoid(ga_ref[...]) * _dot(oa_ref[...], wa_ref[...])
             + _sigmoid(gb_ref[...]) * _dot(ob_ref[...], wb_ref[...])
             + _sigmoid(gc_ref[...]) * _dot(oc_ref[...], wc_ref[...]))
    o_ref[...] = x_ref[...] + _dot(mixed, wo_ref[...])


def _merge(x2d, oa, ob, oc, proj, wa, wb, wc, wo, l, tm):
    M, D = x2d.shape
    row = lambda i: (i, 0)
    wspec = lambda shape: pl.BlockSpec((None,) + shape, lambda i: (l, 0, 0))
    g0 = C_GATE // D
    return pl.pallas_call(
        _merge_kernel,
        grid=(M // tm,),
        in_specs=[
            pl.BlockSpec((tm, D), row),
            pl.BlockSpec((tm, HD), row), pl.BlockSpec((tm, HD), row), pl.BlockSpec((tm, HD), row),
            pl.BlockSpec((tm, D), lambda i: (i, g0)),
            pl.BlockSpec((tm, D), lambda i: (i, g0 + 1)),
            pl.BlockSpec((tm, D), lambda i: (i, g0 + 2)),
            wspec((HD, D)), wspec((HD, D)), wspec((HD, D)), wspec((D, D)),
        ],
        out_specs=pl.BlockSpec((tm, D), row),
        out_shape=jax.ShapeDtypeStruct((M, D), F32),
        compiler_params=_cparams(("parallel",)),
        name="merge",
    )(x2d, oa, ob, oc, proj, proj, proj, wa, wb, wc, wo)


def _ffn_kernel(x_ref, g_ref, win_ref, wout_ref, gf_ref, o_ref, *, dff, tf, final):
    x = x_ref[...]
    h = (x * lax.rsqrt(jnp.mean(x * x, axis=-1, keepdims=True) + EPS) * g_ref[...]).astype(BF16)
    acc = x
    for j in range(dff // tf):
        gg = jnp.dot(h, win_ref[:, j * tf:(j + 1) * tf], preferred_element_type=F32)
        uu = jnp.dot(h, win_ref[:, dff + j * tf:dff + (j + 1) * tf], preferred_element_type=F32)
        act = (_silu(gg) * uu).astype(BF16)
        acc = acc + jnp.dot(act, wout_ref[j * tf:(j + 1) * tf, :], preferred_element_type=F32)
    if final:
        acc = acc * lax.rsqrt(jnp.mean(acc * acc, axis=-1, keepdims=True) + EPS) * gf_ref[...]
    o_ref[...] = acc


def _ffn(x2d, g_all, win, wout, gf, l, tm, final):
    M, D = x2d.shape
    dff = wout.shape[1]
    tf = 256
    return pl.pallas_call(
        functools.partial(_ffn_kernel, dff=dff, tf=tf, final=final),
        grid=(M // tm,),
        in_specs=[
            pl.BlockSpec((tm, D), lambda i: (i, 0)),
            pl.BlockSpec((None, 1, D), lambda i: (l, 0, 0)),
            pl.BlockSpec((None, D, 2 * dff), lambda i: (l, 0, 0)),
            pl.BlockSpec((None, dff, D), lambda i: (l, 0, 0)),
            pl.BlockSpec((1, D), lambda i: (0, 0)),
        ],
        out_specs=pl.BlockSpec((tm, D), lambda i: (i, 0)),
        out_shape=jax.ShapeDtypeStruct((M, D), F32),
        compiler_params=_cparams(("parallel",)),
        name="ffn",
    )(x2d, g_all, win, wout, gf)


def _row_tile(M, cap):
    t = min(cap, M)
    while M % t:
        t //= 2
    return t


def _prep_weights(w_in, w_br_a, w_br_b, w_br_c, w_o, w_ffn_in, w_ffn_out):
    depth, D, _ = w_in.shape
    o_dnqkv = 3 * HD
    o_a = o_dnqkv + 3 * HD
    o_z = o_a + 2 * N_HEADS
    o_hg = o_z + HD
    o_gate = o_hg + 4 * HD
    pad = N_PROJ - (C_AB + 2 * N_HEADS)
    w_perm = jnp.concatenate([
        w_in[:, :, 0:o_a],
        w_in[:, :, o_gate:o_gate + 3 * D],
        w_in[:, :, o_z:o_hg],
        w_in[:, :, o_hg:o_gate],
        w_in[:, :, o_a:o_z],
        jnp.zeros((depth, D, pad), w_in.dtype),
    ], axis=-1).astype(BF16)
    bf = lambda a: a.astype(BF16)
    return w_perm, bf(w_br_a), bf(w_br_b), bf(w_br_c), bf(w_o), bf(w_ffn_in), bf(w_ffn_out)


def _pad_lanes(v, n=LANE):
    return jnp.pad(v, (0, n - v.shape[0]))


def _trunk(x, P, page_table, cache_k, cache_v, dn_conv, dn_S, hg_S):
    B, T, D = x.shape
    depth = P['ln1_g'].shape[0]
    M = B * T
    x2d = x.reshape(M, D)
    tm_proj = _row_tile(M, 1024)
    tm_tok = _row_tile(M, 512)
    W3 = 3 * HD
    dn_r = DN_ROWS if T >= DN_ROWS else min(B, DN_ROWS // T) * T
    ks, vs, bufs, dSs, hSs = [], [], [], [], []
    for l in range(depth):
        proj = _inproj(x2d, P['ln1_g3'], P['w_perm'], l, tm_proj)
        k_new = proj[:, C_SBK:C_SBK + HD].reshape(B, T, N_HEADS, HEAD_DIM)
        v_new = proj[:, C_SBV:C_SBV + HD].reshape(B, T, N_HEADS, HEAD_DIM)
        qkv3 = proj[:, C_DNQKV:C_DNQKV + W3].reshape(B, T, W3)
        if page_table is None:
            o_a = _sb_prompt(proj, P['sb_bias'][l], B, T)
            conv0 = jnp.zeros((B, 8, W3), F32)
            Sb0 = jnp.zeros((B, N_HEADS, HEAD_DIM, HEAD_DIM), F32)
            Sc0 = Sb0
            buf_new = qkv3[:, T - (CONV_W - 1):, :]
        else:
            o_a = _sb_decode(proj, cache_k, cache_v, page_table, P['sb_bias'][l], l, B, T)
            conv0 = jnp.pad(dn_conv[l], ((0, 0), (8 - (CONV_W - 1), 0), (0, 0)))
            Sb0, Sc0 = dn_S[l], hg_S[l]
            buf_new = jnp.concatenate([dn_conv[l], qkv3], axis=1)[:, T:, :]
        abT = proj[:, C_AB:C_AB + 8].T
        alog = P['dn_A_log'][l]
        dtb = P['dn_dt_bias'][l]
        alog_row = _pad_lanes(alog)[None, :]
        dtb_row = _pad_lanes(dtb)[None, :]
        alog_col = jnp.broadcast_to(_pad_lanes(alog, 8)[:, None], (8, dn_r))
        dtb_col = jnp.broadcast_to(_pad_lanes(dtb, 8)[:, None], (8, dn_r))
        o_b, Sb = _deltanet(proj, abT, conv0, Sb0, P['dn_conv_w'][l], alog_row, dtb_row, alog_col,
                            dtb_col, P['dn_norm_g'][l][None, :], B, T)
        o_c, Sc = _hgrn2(proj, Sc0, P['hg_lb_logits'], P['hg_norm_g'][l][None, :], l, B, T)
        x2d = _merge(x2d, o_a, o_b, o_c, proj, P['w_br_a'], P['w_br_b'], P['w_br_c'], P['w_o'], l, tm_tok)
        x2d = _ffn(x2d, P['ln2_g3'], P['w_ffn_in'], P['w_ffn_out'], P['lnf_g'][None, :], l, tm_tok,
                   final=(l == depth - 1))
        ks.append(k_new)
        vs.append(v_new)
        bufs.append(buf_new)
        dSs.append(Sb)
        hSs.append(Sc)
    y = x2d.reshape(B, T, D)
    return y, jnp.stack(ks), jnp.stack(vs), jnp.stack(bufs), jnp.stack(dSs), jnp.stack(hSs)


def kernel(x_prompt, x_sample, cache_sb_k, cache_sb_v, state_dn_conv, state_dn_S, state_hg_S, page_table,
           ln1_g, w_in, sb_bias, dn_conv_w, dn_A_log, dn_dt_bias, dn_norm_g, hg_lb_logits, hg_norm_g,
           w_br_a, w_br_b, w_br_c, w_o, ln2_g, w_ffn_in, w_ffn_out, lnf_g):
    w_perm, wa, wb, wc, wo, wfi, wfo = _prep_weights(w_in, w_br_a, w_br_b, w_br_c, w_o, w_ffn_in, w_ffn_out)
    P = dict(ln1_g3=ln1_g[:, None, :], ln2_g3=ln2_g[:, None, :], w_perm=w_perm, sb_bias=sb_bias,
             dn_conv_w=dn_conv_w, dn_A_log=dn_A_log, dn_dt_bias=dn_dt_bias, dn_norm_g=dn_norm_g,
             hg_lb_logits=hg_lb_logits.astype(F32), hg_norm_g=hg_norm_g, w_br_a=wa, w_br_b=wb, w_br_c=wc,
             w_o=wo, w_ffn_in=wfi, w_ffn_out=wfo, lnf_g=lnf_g, ln1_g=ln1_g)
    depth, n_pool = cache_sb_k.shape[:2]
    ck = cache_sb_k.reshape(depth, n_pool, PAGE_ROWS, HEAD_DIM)
    cv = cache_sb_v.reshape(depth, n_pool, PAGE_ROWS, HEAD_DIM)
    y_p, sbk_p, sbv_p, conv_p, dnS_p, hgS_p = _trunk(x_prompt, P, None, None, None, None, None, None)
    y_s, sbk_s, sbv_s, conv_s, dnS_s, hgS_s = _trunk(x_sample, P, page_table, ck, cv, state_dn_conv,
                                                     state_dn_S, state_hg_S)
    return (y_p, y_s, sbk_p, sbv_p, sbk_s, sbv_s, conv_p, conv_s, dnS_p, dnS_s, hgS_p, hgS_s)
```

```python
import functools
import math

import jax
import jax.numpy as jnp
from jax import lax
from jax.experimental import pallas as pl
from jax.experimental.pallas import tpu as pltpu

F32 = jnp.float32
BF16 = jnp.bfloat16
EPS = 1e-6
LOG2E = math.log2(math.e)

N_HEADS = 4
HEAD_DIM = 128
HD = N_HEADS * HEAD_DIM
CONV_W = 4
PAGE = 128
PAGE_ROWS = PAGE * N_HEADS
CHUNK = 64
SUB = 16
HG_SUB = 16
HG_DIAG = 8
SBP_STREAMS = 4
LANE = 128
DN_ROWS = 128
DN_STEP_ROWS = 256
VMEM_LIMIT = 56 * 1024 * 1024

C_SBQ, C_SBK, C_SBV = 0, 512, 1024
C_DNQKV = 1536
C_GATE = 3072
C_DNZ = 6144
C_HGQ, C_HGF, C_HGI, C_HGZ = 6656, 7168, 7680, 8192
C_AB = 8704
N_PROJ = 8960
PROJ_TN = 1280


def _sigmoid(x):
    return 1.0 / (1.0 + jnp.exp(-x))


def _silu(x):
    return x * _sigmoid(x)


def _softplus(x):
    return jnp.maximum(x, 0.0) + jnp.log(1.0 + jnp.exp(-jnp.abs(x)))


def _dot(a, b):
    return jnp.dot(a.astype(BF16), b.astype(BF16), preferred_element_type=F32)


def _dot_nt(a, b):
    return lax.dot_general(a.astype(BF16), b.astype(BF16), (((1,), (1,)), ((), ())),
                           preferred_element_type=F32)


def _dot_tn(a, b):
    return lax.dot_general(a.astype(BF16), b.astype(BF16), (((0,), (0,)), ((), ())),
                           preferred_element_type=F32)


def _split(a, n):
    out = []
    r = a
    for t in range(n):
        p = r.astype(BF16)
        out.append(p)
        if t + 1 < n:
            r = r - p.astype(F32)
    return out


def _dot_exact_rhs(a, m_bf16, n=3):
    acc = None
    for p in _split(a, n):
        t = jnp.dot(p, m_bf16, preferred_element_type=F32)
        acc = t if acc is None else acc + t
    return acc


def _dot_exact_lhs(m_bf16, a, n=3):
    acc = None
    for p in _split(a, n):
        t = jnp.dot(m_bf16, p, preferred_element_type=F32)
        acc = t if acc is None else acc + t
    return acc


def _dot_hp(a, b, nt=False):
    a1, a2 = _split(a, 2)
    b1, b2 = _split(b, 2)
    if nt:
        f = lambda x, y: lax.dot_general(x, y, (((1,), (1,)), ((), ())), preferred_element_type=F32)
    else:
        f = lambda x, y: jnp.dot(x, y, preferred_element_type=F32)
    return f(a1, b1) + (f(a1, b2) + f(a2, b1))


def _iota2(shape):
    return (lax.broadcasted_iota(jnp.int32, shape, 0), lax.broadcasted_iota(jnp.int32, shape, 1))


def _log2(n):
    k = n.bit_length() - 1
    assert (1 << k) == n
    return k


def _cparams(sem):
    return pltpu.CompilerParams(dimension_semantics=sem, vmem_limit_bytes=VMEM_LIMIT)


def _inproj_kernel(x_ref, g_ref, w_ref, o_ref, h_scr):
    @pl.when(pl.program_id(1) == 0)
    def _():
        x = x_ref[...]
        ms = jnp.mean(x * x, axis=-1, keepdims=True)
        h_scr[...] = (x * lax.rsqrt(ms + EPS) * g_ref[...]).astype(BF16)

    o_ref[...] = jnp.dot(h_scr[...], w_ref[...], preferred_element_type=F32)


def _inproj(x2d, g_all, w_all, l, tm):
    M, D = x2d.shape
    return pl.pallas_call(
        _inproj_kernel,
        grid=(M // tm, N_PROJ // PROJ_TN),
        in_specs=[
            pl.BlockSpec((tm, D), lambda i, j: (i, 0)),
            pl.BlockSpec((None, 1, D), lambda i, j: (l, 0, 0)),
            pl.BlockSpec((None, D, PROJ_TN), lambda i, j: (l, 0, j)),
        ],
        out_specs=pl.BlockSpec((tm, PROJ_TN), lambda i, j: (i, j)),
        out_shape=jax.ShapeDtypeStruct((M, N_PROJ), F32),
        scratch_shapes=[pltpu.VMEM((tm, D), BF16)],
        compiler_params=_cparams(("parallel", "arbitrary")),
        name="inproj",
    )(x2d, g_all, w_all)


def _log2_sig_pair(z2):
    lg = jnp.log2(1.0 + jnp.exp2(-jnp.abs(z2)))
    ls = jnp.minimum(z2, 0.0) - lg
    return ls, ls - z2


def _sbp_blocks(qs, k, v, bias2, U, acc, csum, streams, valids):
    z2 = [_dot_nt(qs[s], k) + bias2 for s in streams]
    pairs = [_log2_sig_pair(z) for z in z2]
    l1ms = [p[1] if m is None else jnp.where(m, p[1], 0.0) for p, m in zip(pairs, valids)]
    splits = [_split(l, 2) for l in l1ms]
    sufs = [jnp.dot(hi, U, preferred_element_type=F32) + jnp.dot(lo, U, preferred_element_type=F32)
            for hi, lo in splits]
    ws = [jnp.exp2(p[0] + suf + csum[s]) for p, suf, s in zip(pairs, sufs, streams)]
    ws = [w if m is None else jnp.where(m, w, 0.0) for w, m in zip(ws, valids)]
    for w, l, s in zip(ws, l1ms, streams):
        acc[s] += _dot(w, v)
        csum[s] += jnp.sum(l, axis=-1, keepdims=True)


def _sbp_kernel(bias_ref, q_ref, k_ref, v_ref, u_ref, o_ref, kbf, vbf, acc, csum, *, tq, nstr):
    h = pl.program_id(1)
    i = pl.program_id(2)

    @pl.when(i == 0)
    def _():
        kbf[...] = k_ref[...].astype(BF16)
        vbf[...] = v_ref[...].astype(BF16)

    bias2 = bias_ref[h] * LOG2E
    q_all = (q_ref[...] * (HEAD_DIM ** -0.5 * LOG2E)).astype(BF16)
    qs = [q_all[s * tq:(s + 1) * tq, :] for s in range(nstr)]
    U = u_ref[...]
    acc[...] = jnp.zeros_like(acc)
    csum[...] = jnp.zeros_like(csum)

    row, col = _iota2((tq, tq))
    causal = col < row

    def kv(kb):
        st = pl.multiple_of(kb * tq, tq)
        return kbf[pl.ds(st, tq), :], vbf[pl.ds(st, tq), :]

    for t in range(nstr):
        d = nstr - 1 - t
        k, v = kv(nstr * i + d)
        streams = list(range(d, nstr))
        _sbp_blocks(qs, k, v, bias2, U, acc, csum, streams, [causal] + [None] * (len(streams) - 1))

    def body(jj, carry):
        k, v = kv(nstr * i - 1 - jj)
        _sbp_blocks(qs, k, v, bias2, U, acc, csum, list(range(nstr)), [None] * nstr)
        return carry

    lax.fori_loop(0, nstr * i, body, 0)
    for s in range(nstr):
        o_ref[s * tq:(s + 1) * tq, :] = acc[s]


def _sb_prompt(proj, bias_l, B, T):
    tq = min(256, T)
    nstr = SBP_STREAMS
    while T % (nstr * tq):
        nstr //= 2
    tqs = nstr * tq
    nq = T // tqs
    r = jnp.arange(tq)
    U = (r[:, None] > r[None, :]).astype(BF16)
    return pl.pallas_call(
        functools.partial(_sbp_kernel, tq=tq, nstr=nstr),
        grid=(B, N_HEADS, nq),
        in_specs=[
            pl.BlockSpec(memory_space=pltpu.SMEM),
            pl.BlockSpec((tqs, HEAD_DIM), lambda b, h, i: (b * nq + i, C_SBQ // HEAD_DIM + h)),
            pl.BlockSpec((T, HEAD_DIM), lambda b, h, i: (b, C_SBK // HEAD_DIM + h)),
            pl.BlockSpec((T, HEAD_DIM), lambda b, h, i: (b, C_SBV // HEAD_DIM + h)),
            pl.BlockSpec((tq, tq), lambda b, h, i: (0, 0)),
        ],
        out_specs=pl.BlockSpec((tqs, HEAD_DIM), lambda b, h, i: (b * nq + i, h)),
        out_shape=jax.ShapeDtypeStruct((B * T, HD), F32),
        scratch_shapes=[
            pltpu.VMEM((T, HEAD_DIM), BF16),
            pltpu.VMEM((T, HEAD_DIM), BF16),
            pltpu.VMEM((nstr, tq, HEAD_DIM), F32),
            pltpu.VMEM((nstr, tq, 1), F32),
        ],
        compiler_params=_cparams(("parallel", "parallel", "arbitrary")),
        name="sb_prompt",
    )(bias_l, proj, proj, proj, U)


def _sbd_group(qr, ks, vs, bias2, U, hm, extra_valid, acc_ref, csum_ref):
    n = len(ks)
    nstr = min(2, n)
    per = n // nstr
    rows = qr.shape[0]
    cat = lambda xs, ax: xs[0] if len(xs) == 1 else jnp.concatenate(xs, axis=ax)
    valid = hm if extra_valid is None else extra_valid
    valid_all = cat([valid] * per, 1)
    groups = [list(range(s * per, (s + 1) * per)) for s in range(nstr)]
    z2s = [_dot_nt(qr, cat([ks[p] for p in g], 0)) + bias2 for g in groups]
    pairs = [_log2_sig_pair(z) for z in z2s]
    l1ms = [jnp.where(valid_all, p[1], 0.0) for p in pairs]
    pages = [[l[:, p * PAGE_ROWS:(p + 1) * PAGE_ROWS] for p in range(per)] for l in l1ms]
    c = csum_ref[...]
    carries = []
    for s in range(nstr):
        cs = []
        for p in range(per):
            cs.append(c)
            c = c + jnp.sum(pages[s][p], axis=-1, keepdims=True)
        carries.append(cs)
    csum_ref[...] = c
    splits = [_split(cat(pg, 0), 2) for pg in pages]
    sufs = [jnp.dot(hi, U, preferred_element_type=F32) + jnp.dot(lo, U, preferred_element_type=F32)
            for hi, lo in splits]
    out = None
    for s in range(nstr):
        ws = []
        for p in range(per):
            e = (pairs[s][0][:, p * PAGE_ROWS:(p + 1) * PAGE_ROWS] + sufs[s][p * rows:(p + 1) * rows, :]
                 + carries[s][p])
            ws.append(jnp.where(valid, jnp.exp2(e), 0.0).astype(BF16))
        t = jnp.dot(cat(ws, 1), cat([vs[p] for p in groups[s]], 0), preferred_element_type=F32)
        out = t if out is None else out + t
    acc_ref[...] += out


def _sbd_kernel(pt_ref, bias_ref, q_ref, kn_ref, vn_ref, *rest, G, n_groups, ts):
    kpages = rest[:G]
    vpages = rest[G:2 * G]
    u_ref, o_ref, qr, kbuf, vbuf, acc, csum = rest[2 * G:]
    del pt_ref
    g = pl.program_id(1)
    rows = N_HEADS * ts
    U = u_ref[...]
    bias2 = bias_ref[:, 0:1] * LOG2E
    row, col = _iota2((rows, PAGE_ROWS))
    hm = (col & (N_HEADS - 1)) == (row // ts)

    @pl.when(g == 0)
    def _():
        acc[...] = jnp.zeros_like(acc)
        csum[...] = jnp.zeros_like(csum)
        q = q_ref[...] * (HEAD_DIM ** -0.5 * LOG2E)
        for hh in range(N_HEADS):
            qr[hh * ts:(hh + 1) * ts, :] = q[:, hh * HEAD_DIM:(hh + 1) * HEAD_DIM].astype(BF16)
        kbuf[...] = jnp.zeros_like(kbuf)
        vbuf[...] = jnp.zeros_like(vbuf)
        kbuf[0:ts * N_HEADS, :] = kn_ref[...].astype(BF16)
        vbuf[0:ts * N_HEADS, :] = vn_ref[...].astype(BF16)
        valid_new = hm & ((col >> 2) < (row % ts))
        _sbd_group(qr[...], [kbuf[...]], [vbuf[...]], bias2, U, hm, valid_new, acc, csum)

    @pl.when(g > 0)
    def _():
        _sbd_group(qr[...], [r[...].astype(BF16) for r in kpages], [r[...].astype(BF16) for r in vpages],
                   bias2, U, hm, None, acc, csum)

    @pl.when(g == n_groups)
    def _():
        for hh in range(N_HEADS):
            o_ref[:, hh * HEAD_DIM:(hh + 1) * HEAD_DIM] = acc[hh * ts:(hh + 1) * ts, :]


def _sb_decode(proj, cache_k, cache_v, page_table, bias_l, l, nb, ts):
    npg = page_table.shape[1]
    G = min(8, npg)
    n_groups = npg // G
    rows = N_HEADS * ts
    r = jnp.arange(PAGE_ROWS)
    U = (((r[:, None] & 3) == (r[None, :] & 3)) & (r[:, None] > r[None, :])).astype(BF16)
    bias_rows = jnp.broadcast_to(jnp.repeat(bias_l, ts)[:, None], (rows, LANE)).astype(F32)
    kn = proj[:, C_SBK:C_SBK + HD].reshape(nb * ts * N_HEADS, HEAD_DIM)
    vn = proj[:, C_SBV:C_SBV + HD].reshape(nb * ts * N_HEADS, HEAD_DIM)

    def page_spec(p):
        def imap(b, g, pt):
            first = jnp.maximum(g - 1, 0) * G
            return (l, pt[b, npg - 1 - (first + p)], 0, 0)
        return pl.BlockSpec((None, None, PAGE_ROWS, HEAD_DIM), imap)

    in_specs = [
        pl.BlockSpec((rows, LANE), lambda b, g, pt: (0, 0)),
        pl.BlockSpec((ts, HD), lambda b, g, pt: (b, C_SBQ // HD)),
        pl.BlockSpec((rows, HEAD_DIM), lambda b, g, pt: (b, 0)),
        pl.BlockSpec((rows, HEAD_DIM), lambda b, g, pt: (b, 0)),
    ] + [page_spec(p) for p in range(G)] + [page_spec(p) for p in range(G)] + [
        pl.BlockSpec((PAGE_ROWS, PAGE_ROWS), lambda b, g, pt: (0, 0)),
    ]
    grid_spec = pltpu.PrefetchScalarGridSpec(
        num_scalar_prefetch=1,
        grid=(nb, n_groups + 1),
        in_specs=in_specs,
        out_specs=pl.BlockSpec((ts, HD), lambda b, g, pt: (b, 0)),
        scratch_shapes=[
            pltpu.VMEM((rows, HEAD_DIM), BF16),
            pltpu.VMEM((PAGE_ROWS, HEAD_DIM), BF16),
            pltpu.VMEM((PAGE_ROWS, HEAD_DIM), BF16),
            pltpu.VMEM((rows, HEAD_DIM), F32),
            pltpu.VMEM((rows, 1), F32),
        ],
    )
    return pl.pallas_call(
        functools.partial(_sbd_kernel, G=G, n_groups=n_groups, ts=ts),
        grid_spec=grid_spec,
        out_shape=jax.ShapeDtypeStruct((nb * ts, HD), F32),
        compiler_params=_cparams(("parallel", "arbitrary")),
        name="sb_decode",
    )(page_table, bias_rows, proj, kn, vn, *([cache_k] * G), *([cache_v] * G), U)


def _unit_lower_inverse(As, C, row, col):
    eye = (row == col).astype(F32)
    blk = min(SUB, C)
    sh = _log2(blk)
    same = (row >> sh) == (col >> sh)
    Ms = [-jnp.where(same, A, 0.0) for A in As]
    Rs = [eye + M for M in Ms]
    for _ in range(sh - 1):
        Ms = [_dot_hp(M, M) for M in Ms]
        Rs = [R + _dot_hp(R, M) for R, M in zip(Rs, Ms)]
    if C > blk:
        Ps = [-_dot_hp(R, jnp.where(same, 0.0, A)) for R, A in zip(Rs, As)]
        Xs = [eye + P for P in Ps]
        for _ in range(_log2(C // blk) - 1):
            Ps = [_dot_hp(P, P) for P in Ps]
            Xs = [X + _dot_hp(X, P) for X, P in zip(Xs, Ps)]
        Rs = [_dot_hp(X, R) for X, R in zip(Xs, Rs)]
    return Rs


def _dn_kernel(qkv_ref, z_ref, ab_ref, abT_ref, conv0_ref, s0_ref, convw_ref, alog_row_ref,
               dtb_row_ref, alog_col_ref, dtb_col_ref, ng_ref, o_ref, sout_ref, xbuf, s_scr,
               *, ns, rs, C, nsteps):
    c = pl.program_id(1)
    R = ns * rs
    npc = rs // C

    @pl.when(c == 0)
    def _():
        xbuf[:, 0:8, :] = conv0_ref[...]
        s_scr[...] = s0_ref[...]

    @pl.when(c > 0)
    def _():
        xbuf[:, 0:8, :] = xbuf[:, rs:rs + 8, :]

    ys = []
    for j in range(ns):
        xbuf[j, 8:8 + rs, :] = qkv_ref[j * rs:(j + 1) * rs, :]
        acc = xbuf[j, 5:5 + rs, :] * convw_ref[0:1, :]
        for t in range(1, CONV_W):
            acc = acc + xbuf[j, 5 + t:5 + t + rs, :] * convw_ref[t:t + 1, :]
        ys.append(acc)
    y = _silu(ys[0] if ns == 1 else jnp.concatenate(ys, axis=0))

    BR = min(R, DN_ROWS)
    nblk = R // BR
    row, col = _iota2((BR, BR))
    shc = _log2(C)
    same_chunk = (row >> shc) == (col >> shc)
    lower_incl = same_chunk & (row >= col)
    tri_lower = lower_incl.astype(BF16)
    tri_upper = (same_chunk & (row <= col)).astype(BF16)
    heads = range(N_HEADS)
    items = [(b, h) for b in range(nblk) for h in heads]
    it = {bh: n for n, bh in enumerate(items)}

    beta_blk, gcol_blk, grow_blk = [], [], []
    for b in range(nblk):
        ab = ab_ref[b * BR:(b + 1) * BR, :]
        g_col = -jnp.exp(alog_row_ref[...]) * _softplus(ab + dtb_row_ref[...])
        beta_blk.append(_sigmoid(ab))
        gcol_blk.append(_dot_exact_lhs(tri_lower, g_col))
        abT = abT_ref[:, b * BR:(b + 1) * BR]
        g_row = -jnp.exp(alog_col_ref[:, 0:BR]) * _softplus(abT + dtb_col_ref[:, 0:BR])
        grow_blk.append(_dot_exact_rhs(g_row, tri_upper))

    ng = ng_ref[...]
    qs, kls, vs, betas, gcs, decays, kbs = [], [], [], [], [], [], []
    for b, h in items:
        yb = y[b * BR:(b + 1) * BR, :]
        q = yb[:, h * HEAD_DIM:(h + 1) * HEAD_DIM]
        k = yb[:, HD + h * HEAD_DIM:HD + (h + 1) * HEAD_DIM]
        vs.append(yb[:, 2 * HD + h * HEAD_DIM:2 * HD + (h + 1) * HEAD_DIM])
        qs.append(q * lax.rsqrt(jnp.sum(q * q, axis=-1, keepdims=True) + EPS) * (HEAD_DIM ** -0.5))
        kls.append(k * lax.rsqrt(jnp.sum(k * k, axis=-1, keepdims=True) + EPS))
        betas.append(beta_blk[b][:, N_HEADS + h:N_HEADS + h + 1])
        gcs.append(gcol_blk[b][:, h:h + 1])
        gr = grow_blk[b][h:h + 1, :]
        decays.append(jnp.exp(jnp.where(lower_incl, gcs[-1] - gr, -jnp.inf)))
        kbs.append(kls[-1] * betas[-1])
    nit = range(len(items))
    As = [jnp.where(row > col, _dot_hp(kbs[n], kls[n], nt=True) * decays[n], 0.0) for n in nit]
    Tms = _unit_lower_inverse(As, C, row, col)
    egs = [jnp.exp(gcs[n]) for n in nit]
    sols = [_dot_hp(Tms[n], jnp.concatenate([vs[n] * betas[n], kbs[n] * egs[n]], axis=-1)) for n in nit]
    us = [s[:, :HEAD_DIM] for s in sols]
    ws = [s[:, HEAD_DIM:] for s in sols]
    attns = [(_dot_nt(qs[n], kls[n]) * decays[n]).astype(BF16) for n in nit]
    qgs = [qs[n] * egs[n] for n in nit]
    v_news = [[] for _ in nit]
    o_inters = [[] for _ in nit]
    for j in range(ns):
        Ss = [s_scr[j, h] for h in heads]
        for i in range(npc):
            g0 = j * rs + i * C
            b, r0 = g0 // BR, g0 % BR
            rows = slice(r0, r0 + C)
            for h in heads:
                n = it[(b, h)]
                v_new = us[n][rows, :] - _dot(ws[n][rows, :], Ss[h])
                o_inters[n].append(_dot(qgs[n][rows, :], Ss[h]))
                g_last = gcs[n][r0 + C - 1:r0 + C, :]
                kg = kls[n][rows, :] * jnp.exp(g_last - gcs[n][rows, :])
                Ss[h] = Ss[h] * jnp.exp(g_last) + _dot_tn(kg, v_new)
                v_news[n].append(v_new)
        for h in heads:
            s_scr[j, h] = Ss[h]
    for n, (b, h) in enumerate(items):
        hs = slice(h * HEAD_DIM, (h + 1) * HEAD_DIM)
        rb = slice(b * BR, (b + 1) * BR)
        v_new_all = v_news[n][0] if len(v_news[n]) == 1 else jnp.concatenate(v_news[n], axis=0)
        o_inter = o_inters[n][0] if len(o_inters[n]) == 1 else jnp.concatenate(o_inters[n], axis=0)
        o = o_inter + _dot(attns[n], v_new_all)
        o = o * lax.rsqrt(jnp.mean(o * o, axis=-1, keepdims=True) + EPS) * ng
        o_ref[rb, hs] = o * _silu(z_ref[rb, hs])

    @pl.when(c == nsteps - 1)
    def _():
        sout_ref[...] = s_scr[...]


def _dn_tiling(B, T):
    if T >= DN_ROWS:
        return 1, (DN_STEP_ROWS if T % DN_STEP_ROWS == 0 else DN_ROWS)
    return min(B, DN_ROWS // T), T


def _deltanet(proj, abT, conv0, S0, convw_l, alog_row, dtb_row, alog_col, dtb_col, ng, B, T):
    C = CHUNK if T % CHUNK == 0 else T
    ns, rs = _dn_tiling(B, T)
    assert T % rs == 0 and rs % C == 0 and B % ns == 0
    R = ns * rs
    nsteps = T // rs
    W3 = 3 * HD
    const = lambda b, c: (0, 0)
    return pl.pallas_call(
        functools.partial(_dn_kernel, ns=ns, rs=rs, C=C, nsteps=nsteps),
        grid=(B // ns, nsteps),
        in_specs=[
            pl.BlockSpec((R, W3), lambda b, c: (b * nsteps + c, C_DNQKV // W3)),
            pl.BlockSpec((R, HD), lambda b, c: (b * nsteps + c, C_DNZ // HD)),
            pl.BlockSpec((R, LANE), lambda b, c: (b * nsteps + c, C_AB // LANE)),
            pl.BlockSpec((8, R), lambda b, c: (0, b * nsteps + c)),
            pl.BlockSpec((ns, 8, W3), lambda b, c: (b, 0, 0)),
            pl.BlockSpec((ns, N_HEADS, HEAD_DIM, HEAD_DIM), lambda b, c: (b, 0, 0, 0)),
            pl.BlockSpec((CONV_W, W3), const),
            pl.BlockSpec((1, LANE), const),
            pl.BlockSpec((1, LANE), const),
            pl.BlockSpec((8, R), const),
            pl.BlockSpec((8, R), const),
            pl.BlockSpec((1, HEAD_DIM), const),
        ],
        out_specs=[
            pl.BlockSpec((R, HD), lambda b, c: (b * nsteps + c, 0)),
            pl.BlockSpec((ns, N_HEADS, HEAD_DIM, HEAD_DIM), lambda b, c: (b, 0, 0, 0)),
        ],
        out_shape=[
            jax.ShapeDtypeStruct((B * T, HD), F32),
            jax.ShapeDtypeStruct((B, N_HEADS, HEAD_DIM, HEAD_DIM), F32),
        ],
        scratch_shapes=[
            pltpu.VMEM((ns, 8 + rs, W3), F32),
            pltpu.VMEM((ns, N_HEADS, HEAD_DIM, HEAD_DIM), F32),
        ],
        compiler_params=_cparams(("parallel", "arbitrary")),
        name="deltanet",
    )(proj, proj, proj, abT, conv0, S0, convw_l, alog_row, dtb_row, alog_col, dtb_col, ng)


def _hg_kernel(q_ref, f_ref, i_ref, z_ref, s0_ref, logit_ref, ng_ref, o_ref, sout_ref,
               kpad, gpad, vpad, st_scr, *, C, nc, layer):
    c = pl.program_id(1)
    sb = min(HG_SUB, C)

    @pl.when(c == 0)
    def _():
        for h in range(N_HEADS):
            st_scr[h] = s0_ref[h].T
        kpad[0:sb, :] = jnp.zeros((sb, HD), F32)
        gpad[0:sb, :] = jnp.zeros((sb, HD), F32)
        vpad[0:sb, :] = jnp.zeros((sb, HD), F32)

    logits = logit_ref[...]
    e = jnp.exp(logits - jnp.max(logits, axis=0, keepdims=True))
    p = e / jnp.sum(e, axis=0, keepdims=True)
    lb = jnp.sum(p[0:layer + 1, :], axis=0, keepdims=True) - p[0:1, :]

    q = _silu(q_ref[...]) * (HEAD_DIM ** -0.5)
    fg = lb + (1.0 - lb) * _sigmoid(f_ref[...])
    k = 1.0 - fg
    v = i_ref[...]
    lf = jnp.log(fg)
    row, col = _iota2((C, C))
    gcum = _dot_exact_lhs((row >= col).astype(BF16), lf)

    kpad[sb:sb + C, :] = k
    gpad[sb:sb + C, :] = gcum
    vpad[sb:sb + C, :] = v

    sd = min(HG_DIAG, sb)
    rpos = lax.broadcasted_iota(jnp.int32, (C, HD), 0) % sd
    o_diag = [jnp.zeros((C, HEAD_DIM), F32) for _ in range(N_HEADS)]
    for dl in range(sd):
        kd = kpad[sb - dl:sb - dl + C, :]
        gd = gpad[sb - dl:sb - dl + C, :]
        vd = vpad[sb - dl:sb - dl + C, :]
        ex = jnp.exp(jnp.where(rpos >= dl, gcum - gd, -jnp.inf))
        t = q * kd * ex
        for h in range(N_HEADS):
            hs = slice(h * HEAD_DIM, (h + 1) * HEAD_DIM)
            s = jnp.sum(t[:, hs], axis=-1, keepdims=True)
            o_diag[h] = o_diag[h] + s * vd[:, hs]

    eg = jnp.exp(gcum)
    g_last = gcum[C - 1:C, :]
    k_tail = k * jnp.exp(g_last - gcum)
    ng = ng_ref[...]
    nsb = C // sb
    heads = range(N_HEADS)
    hsl = [slice(h * HEAD_DIM, (h + 1) * HEAD_DIM) for h in heads]
    STs = [st_scr[h] for h in heads]
    os_ = [_dot_nt(q[:, hsl[h]] * eg[:, hsl[h]], STs[h]) + o_diag[h] for h in heads]
    for h in heads:
        st_scr[h] = STs[h] * jnp.exp(g_last[:, hsl[h]]) + _dot_tn(v[:, hsl[h]], k_tail[:, hsl[h]])
    if sb > sd:
        assert sb == 2 * sd
        nt = C // sd
        gfirst = jnp.broadcast_to(gcum.reshape(nt, sd, HD)[:, 0:1, :], (nt, sd, HD)).reshape(C, HD)
        gnext = jnp.concatenate([gfirst[sd:, :], gfirst[C - sd:, :]], axis=0)
        qt = q * jnp.exp(gcum - gfirst)
        kt = k * jnp.exp(jnp.minimum(gnext - gcum, 0.0))
        shd = _log2(sd)
        pair = (((row >> (shd + 1)) == (col >> (shd + 1)))
                & (((row >> shd) & 1) == 1) & (((col >> shd) & 1) == 0))
        attn = [jnp.where(pair, _dot_nt(qt[:, hsl[h]], kt[:, hsl[h]]), 0.0) for h in heads]
    else:
        attn = [jnp.zeros((C, C), F32) for _ in heads]
    if nsb > 1:
        gblk = jnp.broadcast_to(gcum.reshape(nsb, sb, HD)[:, 0:1, :], (nsb, sb, HD)).reshape(C, HD)
        qt = q * jnp.exp(gcum - gblk)
        shb = _log2(sb)
        for I in range(1, nsb):
            gn = gcum[I * sb:I * sb + 1, :]
            kt = k * jnp.exp(jnp.minimum(gn - gcum, 0.0))
            sel = ((row >> shb) == I) & (col < I * sb)
            attn = [attn[h] + jnp.where(sel, _dot_nt(qt[:, hsl[h]], kt[:, hsl[h]]), 0.0) for h in heads]
    if nsb > 1 or sb > sd:
        os_ = [os_[h] + _dot(attn[h], v[:, hsl[h]]) for h in heads]
    for h in heads:
        o = os_[h]
        o = o * lax.rsqrt(jnp.mean(o * o, axis=-1, keepdims=True) + EPS) * ng
        o_ref[:, hsl[h]] = o * _silu(z_ref[:, hsl[h]])

    @pl.when(c == nc - 1)
    def _():
        for h in range(N_HEADS):
            sout_ref[h] = st_scr[h].T


def _hgrn2(proj, S0, logits, ng, l, B, T):
    C = CHUNK if T % CHUNK == 0 else T
    nc = T // C
    sb = min(HG_SUB, C)
    depth = logits.shape[0]
    const = lambda b, c: (0, 0)
    return pl.pallas_call(
        functools.partial(_hg_kernel, C=C, nc=nc, layer=l),
        grid=(B, nc),
        in_specs=[
            pl.BlockSpec((C, HD), lambda b, c: (b * nc + c, C_HGQ // HD)),
            pl.BlockSpec((C, HD), lambda b, c: (b * nc + c, C_HGF // HD)),
            pl.BlockSpec((C, HD), lambda b, c: (b * nc + c, C_HGI // HD)),
            pl.BlockSpec((C, HD), lambda b, c: (b * nc + c, C_HGZ // HD)),
            pl.BlockSpec((None, N_HEADS, HEAD_DIM, HEAD_DIM), lambda b, c: (b, 0, 0, 0)),
            pl.BlockSpec((depth, HD), const),
            pl.BlockSpec((1, HEAD_DIM), const),
        ],
        out_specs=[
            pl.BlockSpec((C, HD), lambda b, c: (b * nc + c, 0)),
            pl.BlockSpec((None, N_HEADS, HEAD_DIM, HEAD_DIM), lambda b, c: (b, 0, 0, 0)),
        ],
        out_shape=[
            jax.ShapeDtypeStruct((B * T, HD), F32),
            jax.ShapeDtypeStruct((B, N_HEADS, HEAD_DIM, HEAD_DIM), F32),
        ],
        scratch_shapes=[
            pltpu.VMEM((sb + C, HD), F32),
            pltpu.VMEM((sb + C, HD), F32),
            pltpu.VMEM((sb + C, HD), F32),
            pltpu.VMEM((N_HEADS, HEAD_DIM, HEAD_DIM), F32),
        ],
        compiler_params=_cparams(("parallel", "arbitrary")),
        name="hgrn2",
    )(proj, proj, proj, proj, S0, logits, ng)


def _merge_kernel(x_ref, oa_ref, ob_ref, oc_ref, ga_ref, gb_ref, gc_ref, wa_ref, wb_ref, wc_ref,
                  wo_ref, o_ref):
    mixed = (_sigmoid(ga_ref[...]) * _dot(oa_ref[...], wa_ref[...])
             + _sigmoid(gb_ref[...]) * _dot(ob_ref[...], wb_ref[...])
             + _sigmoid(gc_ref[...]) * _dot(oc_ref[...], wc_ref[...]))
    o_ref[...] = x_ref[...] + _dot(mixed, wo_ref[...])


def _merge(x2d, oa, ob, oc, proj, wa, wb, wc, wo, l, tm):
    M, D = x2d.shape
    row = lambda i: (i, 0)
    wspec = lambda shape: pl.BlockSpec((None,) + shape, lambda i: (l, 0, 0))
    g0 = C_GATE // D
    return pl.pallas_call(
        _merge_kernel,
        grid=(M // tm,),
        in_specs=[
            pl.BlockSpec((tm, D), row),
            pl.BlockSpec((tm, HD), row), pl.BlockSpec((tm, HD), row), pl.BlockSpec((tm, HD), row),
            pl.BlockSpec((tm, D), lambda i: (i, g0)),
            pl.BlockSpec((tm, D), lambda i: (i, g0 + 1)),
            pl.BlockSpec((tm, D), lambda i: (i, g0 + 2)),
            wspec((HD, D)), wspec((HD, D)), wspec((HD, D)), wspec((D, D)),
        ],
        out_specs=pl.BlockSpec((tm, D), row),
        out_shape=jax.ShapeDtypeStruct((M, D), F32),
        compiler_params=_cparams(("parallel",)),
        name="merge",
    )(x2d, oa, ob, oc, proj, proj, proj, wa, wb, wc, wo)


def _ffn_kernel(x_ref, g_ref, win_ref, wout_ref, gf_ref, o_ref, *, dff, tf, final):
    x = x_ref[...]
    h = (x * lax.rsqrt(jnp.mean(x * x, axis=-1, keepdims=True) + EPS) * g_ref[...]).astype(BF16)
    acc = x
    for j in range(dff // tf):
        gg = jnp.dot(h, win_ref[:, j * tf:(j + 1) * tf], preferred_element_type=F32)
        uu = jnp.dot(h, win_ref[:, dff + j * tf:dff + (j + 1) * tf], preferred_element_type=F32)
        act = (_silu(gg) * uu).astype(BF16)
        acc = acc + jnp.dot(act, wout_ref[j * tf:(j + 1) * tf, :], preferred_element_type=F32)
    if final:
        acc = acc * lax.rsqrt(jnp.mean(acc * acc, axis=-1, keepdims=True) + EPS) * gf_ref[...]
    o_ref[...] = acc


def _ffn(x2d, g_all, win, wout, gf, l, tm, final):
    M, D = x2d.shape
    dff = wout.shape[1]
    tf = 256
    return pl.pallas_call(
        functools.partial(_ffn_kernel, dff=dff, tf=tf, final=final),
        grid=(M // tm,),
        in_specs=[
            pl.BlockSpec((tm, D), lambda i: (i, 0)),
            pl.BlockSpec((None, 1, D), lambda i: (l, 0, 0)),
            pl.BlockSpec((None, D, 2 * dff), lambda i: (l, 0, 0)),
            pl.BlockSpec((None, dff, D), lambda i: (l, 0, 0)),
            pl.BlockSpec((1, D), lambda i: (0, 0)),
        ],
        out_specs=pl.BlockSpec((tm, D), lambda i: (i, 0)),
        out_shape=jax.ShapeDtypeStruct((M, D), F32),
        compiler_params=_cparams(("parallel",)),
        name="ffn",
    )(x2d, g_all, win, wout, gf)


def _row_tile(M, cap):
    t = min(cap, M)
    while M % t:
        t //= 2
    return t


def _prep_weights(w_in, w_br_a, w_br_b, w_br_c, w_o, w_ffn_in, w_ffn_out):
    depth, D, _ = w_in.shape
    o_dnqkv = 3 * HD
    o_a = o_dnqkv + 3 * HD
    o_z = o_a + 2 * N_HEADS
    o_hg = o_z + HD
    o_gate = o_hg + 4 * HD
    pad = N_PROJ - (C_AB + 2 * N_HEADS)
    w_perm = jnp.concatenate([
        w_in[:, :, 0:o_a],
        w_in[:, :, o_gate:o_gate + 3 * D],
        w_in[:, :, o_z:o_hg],
        w_in[:, :, o_hg:o_gate],
        w_in[:, :, o_a:o_z],
        jnp.zeros((depth, D, pad), w_in.dtype),
    ], axis=-1).astype(BF16)
    bf = lambda a: a.astype(BF16)
    return w_perm, bf(w_br_a), bf(w_br_b), bf(w_br_c), bf(w_o), bf(w_ffn_in), bf(w_ffn_out)


def _pad_lanes(v, n=LANE):
    return jnp.pad(v, (0, n - v.shape[0]))


def _trunk(x, P, page_table, cache_k, cache_v, dn_conv, dn_S, hg_S):
    B, T, D = x.shape
    depth = P['ln1_g'].shape[0]
    M = B * T
    x2d = x.reshape(M, D)
    tm_proj = _row_tile(M, 1024)
    tm_tok = _row_tile(M, 512)
    W3 = 3 * HD
    dn_ns, dn_rs = _dn_tiling(B, T)
    dn_r = dn_ns * dn_rs
    ks, vs, bufs, dSs, hSs = [], [], [], [], []
    for l in range(depth):
        proj = _inproj(x2d, P['ln1_g3'], P['w_perm'], l, tm_proj)
        k_new = proj[:, C_SBK:C_SBK + HD].reshape(B, T, N_HEADS, HEAD_DIM)
        v_new = proj[:, C_SBV:C_SBV + HD].reshape(B, T, N_HEADS, HEAD_DIM)
        qkv3 = proj[:, C_DNQKV:C_DNQKV + W3].reshape(B, T, W3)
        if page_table is None:
            o_a = _sb_prompt(proj, P['sb_bias'][l], B, T)
            conv0 = jnp.zeros((B, 8, W3), F32)
            Sb0 = jnp.zeros((B, N_HEADS, HEAD_DIM, HEAD_DIM), F32)
            Sc0 = Sb0
            buf_new = qkv3[:, T - (CONV_W - 1):, :]
        else:
            o_a = _sb_decode(proj, cache_k, cache_v, page_table, P['sb_bias'][l], l, B, T)
            conv0 = jnp.pad(dn_conv[l], ((0, 0), (8 - (CONV_W - 1), 0), (0, 0)))
            Sb0, Sc0 = dn_S[l], hg_S[l]
            buf_new = jnp.concatenate([dn_conv[l], qkv3], axis=1)[:, T:, :]
        abT = proj[:, C_AB:C_AB + 8].T
        alog = P['dn_A_log'][l]
        dtb = P['dn_dt_bias'][l]
        alog_row = _pad_lanes(alog)[None, :]
        dtb_row = _pad_lanes(dtb)[None, :]
        alog_col = jnp.broadcast_to(_pad_lanes(alog, 8)[:, None], (8, dn_r))
        dtb_col = jnp.broadcast_to(_pad_lanes(dtb, 8)[:, None], (8, dn_r))
        o_b, Sb = _deltanet(proj, abT, conv0, Sb0, P['dn_conv_w'][l], alog_row, dtb_row, alog_col,
                            dtb_col, P['dn_norm_g'][l][None, :], B, T)
        o_c, Sc = _hgrn2(proj, Sc0, P['hg_lb_logits'], P['hg_norm_g'][l][None, :], l, B, T)
        x2d = _merge(x2d, o_a, o_b, o_c, proj, P['w_br_a'], P['w_br_b'], P['w_br_c'], P['w_o'], l, tm_tok)
        x2d = _ffn(x2d, P['ln2_g3'], P['w_ffn_in'], P['w_ffn_out'], P['lnf_g'][None, :], l, tm_tok,
                   final=(l == depth - 1))
        ks.append(k_new)
        vs.append(v_new)
        bufs.append(buf_new)
        dSs.append(Sb)
        hSs.append(Sc)
    y = x2d.reshape(B, T, D)
    return y, jnp.stack(ks), jnp.stack(vs), jnp.stack(bufs), jnp.stack(dSs), jnp.stack(hSs)


def kernel(x_prompt, x_sample, cache_sb_k, cache_sb_v, state_dn_conv, state_dn_S, state_hg_S, page_table,
           ln1_g, w_in, sb_bias, dn_conv_w, dn_A_log, dn_dt_bias, dn_norm_g, hg_lb_logits, hg_norm_g,
           w_br_a, w_br_b, w_br_c, w_o, ln2_g, w_ffn_in, w_ffn_out, lnf_g):
    w_perm, wa, wb, wc, wo, wfi, wfo = _prep_weights(w_in, w_br_a, w_br_b, w_br_c, w_o, w_ffn_in, w_ffn_out)
    P = dict(ln1_g3=ln1_g[:, None, :], ln2_g3=ln2_g[:, None, :], w_perm=w_perm, sb_bias=sb_bias,
             dn_conv_w=dn_conv_w, dn_A_log=dn_A_log, dn_dt_bias=dn_dt_bias, dn_norm_g=dn_norm_g,
             hg_lb_logits=hg_lb_logits.astype(F32), hg_norm_g=hg_norm_g, w_br_a=wa, w_br_b=wb, w_br_c=wc,
             w_o=wo, w_ffn_in=wfi, w_ffn_out=wfo, lnf_g=lnf_g, ln1_g=ln1_g)
    depth, n_pool = cache_sb_k.shape[:2]
    ck = cache_sb_k.reshape(depth, n_pool, PAGE_ROWS, HEAD_DIM)
    cv = cache_sb_v.reshape(depth, n_pool, PAGE_ROWS, HEAD_DIM)
    y_p, sbk_p, sbv_p, conv_p, dnS_p, hgS_p = _trunk(x_prompt, P, None, None, None, None, None, None)
    y_s, sbk_s, sbv_s, conv_s, dnS_s, hgS_s = _trunk(x_sample, P, page_table, ck, cv, state_dn_conv,
                                                     state_dn_S, state_hg_S)
    return (y_p, y_s, sbk_p, sbv_p, sbk_s, sbv_s, conv_p, conv_s, dnS_p, dnS_s, hgS_p, hgS_s)
```

```python
import functools
import math

import jax
import jax.numpy as jnp
from jax import lax
from jax.experimental import pallas as pl
from jax.experimental.pallas import tpu as pltpu

F32 = jnp.float32
BF16 = jnp.bfloat16
EPS = 1e-6
LOG2E = math.log2(math.e)

N_HEADS = 4
HEAD_DIM = 128
HD = N_HEADS * HEAD_DIM
CONV_W = 4
PAGE = 128
PAGE_ROWS = PAGE * N_HEADS
CHUNK = 64
SUB = 16
HG_SUB = 16
HG_DIAG = 4
SBP_STREAMS = 4
LANE = 128
DN_ROWS = 128
DN_STEP_ROWS = 256
VMEM_LIMIT = 56 * 1024 * 1024

C_SBK, C_SBV, C_SBQ = 0, 512, 1024
C_DNQKV = 1536
C_GATE = 3072
C_DNZ = 6144
C_HGQ, C_HGF, C_HGI, C_HGZ = 6656, 7168, 7680, 8192
C_AB = 8704
N_PROJ = 9216
PROJ_TN = 1536
DEC_PAGES = 16


def _sigmoid(x):
    return 1.0 / (1.0 + jnp.exp(-x))


def _silu(x):
    return x * _sigmoid(x)


def _softplus(x):
    return jnp.maximum(x, 0.0) + jnp.log(1.0 + jnp.exp(-jnp.abs(x)))


def _dot(a, b):
    return jnp.dot(a.astype(BF16), b.astype(BF16), preferred_element_type=F32)


def _dot_nt(a, b):
    return lax.dot_general(a.astype(BF16), b.astype(BF16), (((1,), (1,)), ((), ())),
                           preferred_element_type=F32)


def _dot_tn(a, b):
    return lax.dot_general(a.astype(BF16), b.astype(BF16), (((0,), (0,)), ((), ())),
                           preferred_element_type=F32)


def _split(a, n):
    out = []
    r = a
    for t in range(n):
        p = r.astype(BF16)
        out.append(p)
        if t + 1 < n:
            r = r - p.astype(F32)
    return out


def _split_trunc(a):
    bits = lax.bitcast_convert_type(a, jnp.uint32) & jnp.uint32(0xFFFF0000)
    hi = lax.bitcast_convert_type(bits, F32)
    return hi.astype(BF16), (a - hi).astype(BF16)


def _dot_exact_rhs(a, m_bf16, n=3):
    acc = None
    for p in _split(a, n):
        t = jnp.dot(p, m_bf16, preferred_element_type=F32)
        acc = t if acc is None else acc + t
    return acc


def _dot_exact_lhs(m_bf16, a, n=3):
    acc = None
    for p in _split(a, n):
        t = jnp.dot(m_bf16, p, preferred_element_type=F32)
        acc = t if acc is None else acc + t
    return acc


def _dot_hp(a, b, nt=False):
    a1, a2 = _split(a, 2)
    b1, b2 = _split(b, 2)
    if nt:
        f = lambda x, y: lax.dot_general(x, y, (((1,), (1,)), ((), ())), preferred_element_type=F32)
    else:
        f = lambda x, y: jnp.dot(x, y, preferred_element_type=F32)
    return f(a1, b1) + (f(a1, b2) + f(a2, b1))


def _iota2(shape):
    return (lax.broadcasted_iota(jnp.int32, shape, 0), lax.broadcasted_iota(jnp.int32, shape, 1))


def _log2(n):
    k = n.bit_length() - 1
    assert (1 << k) == n
    return k


def _cparams(sem):
    return pltpu.CompilerParams(dimension_semantics=sem, vmem_limit_bytes=VMEM_LIMIT)


def _inproj_kernel(x_ref, g_ref, w_ref, kin_ref, vin_ref, o_ref, kst_ref, vst_ref, h_scr):
    del kin_ref, vin_ref
    j = pl.program_id(1)

    @pl.when(j == 0)
    def _():
        x = x_ref[...]
        ms = jnp.mean(x * x, axis=-1, keepdims=True)
        h_scr[...] = (x * lax.rsqrt(ms + EPS) * g_ref[...]).astype(BF16)

    res = jnp.dot(h_scr[...], w_ref[...], preferred_element_type=F32)
    o_ref[...] = res

    @pl.when(j == 0)
    def _():
        kst_ref[...] = res[:, C_SBK:C_SBK + HD]
        vst_ref[...] = res[:, C_SBV:C_SBV + HD]


def _inproj(x2d, g_all, w_all, kst, vst, l, tm):
    M, D = x2d.shape
    depth = kst.shape[0]
    kv_spec = pl.BlockSpec((None, tm, HD), lambda i, j: (l, i, 0))
    return pl.pallas_call(
        _inproj_kernel,
        grid=(M // tm, N_PROJ // PROJ_TN),
        in_specs=[
            pl.BlockSpec((tm, D), lambda i, j: (i, 0)),
            pl.BlockSpec((None, 1, D), lambda i, j: (l, 0, 0)),
            pl.BlockSpec((None, D, PROJ_TN), lambda i, j: (l, 0, j)),
            pl.BlockSpec(memory_space=pl.ANY),
            pl.BlockSpec(memory_space=pl.ANY),
        ],
        out_specs=[pl.BlockSpec((tm, PROJ_TN), lambda i, j: (i, j)), kv_spec, kv_spec],
        out_shape=[jax.ShapeDtypeStruct((M, N_PROJ), F32),
                   jax.ShapeDtypeStruct((depth, M, HD), F32),
                   jax.ShapeDtypeStruct((depth, M, HD), F32)],
        input_output_aliases={3: 1, 4: 2},
        scratch_shapes=[pltpu.VMEM((tm, D), BF16)],
        compiler_params=_cparams(("parallel", "arbitrary")),
        name="inproj",
    )(x2d, g_all, w_all, kst, vst)


def _log2_sig_pair(z2):
    lg = jnp.log2(1.0 + jnp.exp2(-jnp.abs(z2)))
    ls = jnp.minimum(z2, 0.0) - lg
    return ls, ls - z2


def _sbp_blocks(qs, k, v, bias2, U, acc, csum, streams, valids):
    z2 = [_dot_nt(qs[s], k) + bias2 for s in streams]
    pairs = [_log2_sig_pair(z) for z in z2]
    l1ms = [p[1] if m is None else jnp.where(m, p[1], 0.0) for p, m in zip(pairs, valids)]
    splits = [_split_trunc(l) for l in l1ms]
    sufs = [jnp.dot(jnp.concatenate([hi, lo], axis=1), U, preferred_element_type=F32) for hi, lo in splits]
    ws = [jnp.exp2(p[0] + suf + csum[s]) for p, suf, s in zip(pairs, sufs, streams)]
    ws = [w if m is None else jnp.where(m, w, 0.0) for w, m in zip(ws, valids)]
    for w, l, s in zip(ws, l1ms, streams):
        acc[s] += _dot(w, v)
        csum[s] += jnp.sum(l, axis=-1, keepdims=True)


def _sbp_kernel(bias_ref, q_ref, k_ref, v_ref, u_ref, o_ref, kbf, vbf, acc, csum, *, tq, nstr):
    h = pl.program_id(1)
    i = pl.program_id(2)

    @pl.when(i == 0)
    def _():
        kbf[...] = k_ref[...].astype(BF16)
        vbf[...] = v_ref[...].astype(BF16)

    bias2 = bias_ref[h] * LOG2E
    q_all = (q_ref[...] * (HEAD_DIM ** -0.5 * LOG2E)).astype(BF16)
    qs = [q_all[s * tq:(s + 1) * tq, :] for s in range(nstr)]
    U = u_ref[...]
    acc[...] = jnp.zeros_like(acc)
    csum[...] = jnp.zeros_like(csum)

    row, col = _iota2((tq, tq))
    causal = col < row

    def kv(kb):
        st = pl.multiple_of(kb * tq, tq)
        return kbf[pl.ds(st, tq), :], vbf[pl.ds(st, tq), :]

    for t in range(nstr):
        d = nstr - 1 - t
        k, v = kv(nstr * i + d)
        streams = list(range(d, nstr))
        _sbp_blocks(qs, k, v, bias2, U, acc, csum, streams, [causal] + [None] * (len(streams) - 1))

    def body(jj, carry):
        k, v = kv(nstr * i - 1 - jj)
        _sbp_blocks(qs, k, v, bias2, U, acc, csum, list(range(nstr)), [None] * nstr)
        return carry

    lax.fori_loop(0, nstr * i, body, 0)
    for s in range(nstr):
        o_ref[s * tq:(s + 1) * tq, :] = acc[s]


def _sb_prompt(proj, bias_l, B, T):
    tq = min(256, T)
    nstr = SBP_STREAMS
    while T % (nstr * tq):
        nstr //= 2
    tqs = nstr * tq
    nq = T // tqs
    r = jnp.arange(tq)
    U1 = (r[:, None] > r[None, :]).astype(BF16)
    U = jnp.concatenate([U1, U1], axis=0)
    return pl.pallas_call(
        functools.partial(_sbp_kernel, tq=tq, nstr=nstr),
        grid=(B, N_HEADS, nq),
        in_specs=[
            pl.BlockSpec(memory_space=pltpu.SMEM),
            pl.BlockSpec((tqs, HEAD_DIM), lambda b, h, i: (b * nq + i, C_SBQ // HEAD_DIM + h)),
            pl.BlockSpec((T, HEAD_DIM), lambda b, h, i: (b, C_SBK // HEAD_DIM + h)),
            pl.BlockSpec((T, HEAD_DIM), lambda b, h, i: (b, C_SBV // HEAD_DIM + h)),
            pl.BlockSpec((2 * tq, tq), lambda b, h, i: (0, 0)),
        ],
        out_specs=pl.BlockSpec((tqs, HEAD_DIM), lambda b, h, i: (b * nq + i, h)),
        out_shape=jax.ShapeDtypeStruct((B * T, HD), F32),
        scratch_shapes=[
            pltpu.VMEM((T, HEAD_DIM), BF16),
            pltpu.VMEM((T, HEAD_DIM), BF16),
            pltpu.VMEM((nstr, tq, HEAD_DIM), F32),
            pltpu.VMEM((nstr, tq, 1), F32),
        ],
        compiler_params=_cparams(("parallel", "parallel", "arbitrary")),
        name="sb_prompt",
    )(bias_l, proj, proj, proj, U)


def _sbd_group(qr, ks, vs, bias2, U, hm, extra_valid, acc_ref, csum_ref):
    n = len(ks)
    nstr = min(2, n)
    per = n // nstr
    rows = qr.shape[0]
    cat = lambda xs, ax: xs[0] if len(xs) == 1 else jnp.concatenate(xs, axis=ax)
    valid = hm if extra_valid is None else extra_valid
    valid_all = cat([valid] * per, 1)
    groups = [list(range(s * per, (s + 1) * per)) for s in range(nstr)]
    z2s = [_dot_nt(qr, cat([ks[p] for p in g], 0)) + bias2 for g in groups]
    pairs = [_log2_sig_pair(z) for z in z2s]
    l1ms = [jnp.where(valid_all, p[1], 0.0) for p in pairs]
    pages = [[l[:, p * PAGE_ROWS:(p + 1) * PAGE_ROWS] for p in range(per)] for l in l1ms]
    c = csum_ref[...]
    carries = []
    for s in range(nstr):
        cs = []
        for p in range(per):
            cs.append(c)
            c = c + jnp.sum(pages[s][p], axis=-1, keepdims=True)
        carries.append(cs)
    csum_ref[...] = c
    splits = [_split(cat(pg, 0), 2) for pg in pages]
    sufs = [jnp.dot(hi, U, preferred_element_type=F32) + jnp.dot(lo, U, preferred_element_type=F32)
            for hi, lo in splits]
    out = None
    for s in range(nstr):
        ws = []
        for p in range(per):
            e = (pairs[s][0][:, p * PAGE_ROWS:(p + 1) * PAGE_ROWS] + sufs[s][p * rows:(p + 1) * rows, :]
                 + carries[s][p])
            ws.append(jnp.where(valid, jnp.exp2(e), 0.0).astype(BF16))
        t = jnp.dot(cat(ws, 1), cat([vs[p] for p in groups[s]], 0), preferred_element_type=F32)
        out = t if out is None else out + t
    acc_ref[...] += out


def _sbd_kernel(pt_ref, bias_ref, q_ref, kn_ref, vn_ref, *rest, G, n_groups, ts):
    kpages = rest[:G]
    vpages = rest[G:2 * G]
    u_ref, o_ref, qr, kbuf, vbuf, acc, csum = rest[2 * G:]
    del pt_ref
    g = pl.program_id(1)
    rows = N_HEADS * ts
    U = u_ref[...]
    bias2 = bias_ref[:, 0:1] * LOG2E
    row, col = _iota2((rows, PAGE_ROWS))
    hm = (col & (N_HEADS - 1)) == (row // ts)

    @pl.when(g == 0)
    def _():
        acc[...] = jnp.zeros_like(acc)
        csum[...] = jnp.zeros_like(csum)
        q = q_ref[...] * (HEAD_DIM ** -0.5 * LOG2E)
        for hh in range(N_HEADS):
            qr[hh * ts:(hh + 1) * ts, :] = q[:, hh * HEAD_DIM:(hh + 1) * HEAD_DIM].astype(BF16)
        kbuf[...] = jnp.zeros_like(kbuf)
        vbuf[...] = jnp.zeros_like(vbuf)
        kbuf[0:ts * N_HEADS, :] = kn_ref[...].astype(BF16)
        vbuf[0:ts * N_HEADS, :] = vn_ref[...].astype(BF16)
        valid_new = hm & ((col >> 2) < (row % ts))
        _sbd_group(qr[...], [kbuf[...]], [vbuf[...]], bias2, U, hm, valid_new, acc, csum)

    @pl.when(g > 0)
    def _():
        _sbd_group(qr[...], [r[...].astype(BF16) for r in kpages], [r[...].astype(BF16) for r in vpages],
                   bias2, U, hm, None, acc, csum)

    @pl.when(g == n_groups)
    def _():
        for hh in range(N_HEADS):
            o_ref[:, hh * HEAD_DIM:(hh + 1) * HEAD_DIM] = acc[hh * ts:(hh + 1) * ts, :]


def _sb_decode(proj, cache_k, cache_v, page_table, bias_l, l, nb, ts):
    npg = page_table.shape[1]
    G = min(DEC_PAGES, npg)
    n_groups = npg // G
    rows = N_HEADS * ts
    r = jnp.arange(PAGE_ROWS)
    U = (((r[:, None] & 3) == (r[None, :] & 3)) & (r[:, None] > r[None, :])).astype(BF16)
    bias_rows = jnp.broadcast_to(jnp.repeat(bias_l, ts)[:, None], (rows, LANE)).astype(F32)
    kn = proj[:, C_SBK:C_SBK + HD].reshape(nb * ts * N_HEADS, HEAD_DIM)
    vn = proj[:, C_SBV:C_SBV + HD].reshape(nb * ts * N_HEADS, HEAD_DIM)

    def page_spec(p):
        def imap(b, g, pt):
            first = jnp.maximum(g - 1, 0) * G
            return (l, pt[b, npg - 1 - (first + p)], 0, 0)
        return pl.BlockSpec((None, None, PAGE_ROWS, HEAD_DIM), imap)

    in_specs = [
        pl.BlockSpec((rows, LANE), lambda b, g, pt: (0, 0)),
        pl.BlockSpec((ts, HD), lambda b, g, pt: (b, C_SBQ // HD)),
        pl.BlockSpec((rows, HEAD_DIM), lambda b, g, pt: (b, 0)),
        pl.BlockSpec((rows, HEAD_DIM), lambda b, g, pt: (b, 0)),
    ] + [page_spec(p) for p in range(G)] + [page_spec(p) for p in range(G)] + [
        pl.BlockSpec((PAGE_ROWS, PAGE_ROWS), lambda b, g, pt: (0, 0)),
    ]
    grid_spec = pltpu.PrefetchScalarGridSpec(
        num_scalar_prefetch=1,
        grid=(nb, n_groups + 1),
        in_specs=in_specs,
        out_specs=pl.BlockSpec((ts, HD), lambda b, g, pt: (b, 0)),
        scratch_shapes=[
            pltpu.VMEM((rows, HEAD_DIM), BF16),
            pltpu.VMEM((PAGE_ROWS, HEAD_DIM), BF16),
            pltpu.VMEM((PAGE_ROWS, HEAD_DIM), BF16),
            pltpu.VMEM((rows, HEAD_DIM), F32),
            pltpu.VMEM((rows, 1), F32),
        ],
    )
    return pl.pallas_call(
        functools.partial(_sbd_kernel, G=G, n_groups=n_groups, ts=ts),
        grid_spec=grid_spec,
        out_shape=jax.ShapeDtypeStruct((nb * ts, HD), F32),
        compiler_params=_cparams(("parallel", "arbitrary")),
        name="sb_decode",
    )(page_table, bias_rows, proj, kn, vn, *([cache_k] * G), *([cache_v] * G), U)


def _unit_lower_inverse(As, C, row, col):
    eye = (row == col).astype(F32)
    blk = min(SUB, C)
    sh = _log2(blk)
    same = (row >> sh) == (col >> sh)
    Ms = [-jnp.where(same, A, 0.0) for A in As]
    Rs = [eye + M for M in Ms]
    for _ in range(sh - 1):
        Ms = [_dot_hp(M, M) for M in Ms]
        Rs = [R + _dot_hp(R, M) for R, M in zip(Rs, Ms)]
    if C > blk:
        Ps = [-_dot_hp(R, jnp.where(same, 0.0, A)) for R, A in zip(Rs, As)]
        Xs = [eye + P for P in Ps]
        for _ in range(_log2(C // blk) - 1):
            Ps = [_dot_hp(P, P) for P in Ps]
            Xs = [X + _dot_hp(X, P) for X, P in zip(Xs, Ps)]
        Rs = [_dot_hp(X, R) for X, R in zip(Xs, Rs)]
    return Rs


def _dn_kernel(qkv_ref, z_ref, ab_ref, abT_ref, conv0_ref, s0_ref, convw_ref, alog_row_ref,
               dtb_row_ref, alog_col_ref, dtb_col_ref, ng_ref, o_ref, sout_ref, xbuf, s_scr,
               *, ns, rs, C, nsteps):
    c = pl.program_id(1)
    R = ns * rs
    npc = rs // C

    @pl.when(c == 0)
    def _():
        xbuf[:, 0:8, :] = conv0_ref[...]
        s_scr[...] = s0_ref[...]

    @pl.when(c > 0)
    def _():
        xbuf[:, 0:8, :] = xbuf[:, rs:rs + 8, :]

    ys = []
    for j in range(ns):
        xbuf[j, 8:8 + rs, :] = qkv_ref[j * rs:(j + 1) * rs, :]
        acc = xbuf[j, 5:5 + rs, :] * convw_ref[0:1, :]
        for t in range(1, CONV_W):
            acc = acc + xbuf[j, 5 + t:5 + t + rs, :] * convw_ref[t:t + 1, :]
        ys.append(acc)
    y = _silu(ys[0] if ns == 1 else jnp.concatenate(ys, axis=0))

    BR = min(R, DN_ROWS)
    nblk = R // BR
    row, col = _iota2((BR, BR))
    shc = _log2(C)
    same_chunk = (row >> shc) == (col >> shc)
    lower_incl = same_chunk & (row >= col)
    tri_lower = lower_incl.astype(BF16)
    tri_upper = (same_chunk & (row <= col)).astype(BF16)
    heads = range(N_HEADS)
    items = [(b, h) for b in range(nblk) for h in heads]
    it = {bh: n for n, bh in enumerate(items)}

    beta_blk, gcol_blk, grow_blk = [], [], []
    for b in range(nblk):
        ab = ab_ref[b * BR:(b + 1) * BR, :]
        g_col = -jnp.exp(alog_row_ref[...]) * _softplus(ab + dtb_row_ref[...])
        beta_blk.append(_sigmoid(ab))
        gcol_blk.append(_dot_exact_lhs(tri_lower, g_col))
        abT = abT_ref[:, b * BR:(b + 1) * BR]
        g_row = -jnp.exp(alog_col_ref[:, 0:BR]) * _softplus(abT + dtb_col_ref[:, 0:BR])
        grow_blk.append(_dot_exact_rhs(g_row, tri_upper))

    ng = ng_ref[...]
    qs, kls, vs, betas, gcs, decays, kbs = [], [], [], [], [], [], []
    for b, h in items:
        yb = y[b * BR:(b + 1) * BR, :]
        q = yb[:, h * HEAD_DIM:(h + 1) * HEAD_DIM]
        k = yb[:, HD + h * HEAD_DIM:HD + (h + 1) * HEAD_DIM]
        vs.append(yb[:, 2 * HD + h * HEAD_DIM:2 * HD + (h + 1) * HEAD_DIM])
        qs.append(q * lax.rsqrt(jnp.sum(q * q, axis=-1, keepdims=True) + EPS) * (HEAD_DIM ** -0.5))
        kls.append(k * lax.rsqrt(jnp.sum(k * k, axis=-1, keepdims=True) + EPS))
        betas.append(beta_blk[b][:, N_HEADS + h:N_HEADS + h + 1])
        gcs.append(gcol_blk[b][:, h:h + 1])
        gr = grow_blk[b][h:h + 1, :]
        decays.append(jnp.exp(jnp.where(lower_incl, gcs[-1] - gr, -jnp.inf)))
        kbs.append(kls[-1] * betas[-1])
    nit = range(len(items))
    As = [jnp.where(row > col, _dot_hp(kbs[n], kls[n], nt=True) * decays[n], 0.0) for n in nit]
    Tms = _unit_lower_inverse(As, C, row, col)
    egs = [jnp.exp(gcs[n]) for n in nit]
    sols = [_dot_hp(Tms[n], jnp.concatenate([vs[n] * betas[n], kbs[n] * egs[n]], axis=-1)) for n in nit]
    us = [s[:, :HEAD_DIM] for s in sols]
    ws = [s[:, HEAD_DIM:] for s in sols]
    attns = [(_dot_nt(qs[n], kls[n]) * decays[n]).astype(BF16) for n in nit]
    qgs = [qs[n] * egs[n] for n in nit]
    v_news = [[] for _ in nit]
    o_inters = [[] for _ in nit]
    for j in range(ns):
        Ss = [s_scr[j, h] for h in heads]
        for i in range(npc):
            g0 = j * rs + i * C
            b, r0 = g0 // BR, g0 % BR
            rows = slice(r0, r0 + C)
            for h in heads:
                n = it[(b, h)]
                v_new = us[n][rows, :] - _dot(ws[n][rows, :], Ss[h])
                o_inters[n].append(_dot(qgs[n][rows, :], Ss[h]))
                g_last = gcs[n][r0 + C - 1:r0 + C, :]
                kg = kls[n][rows, :] * jnp.exp(g_last - gcs[n][rows, :])
                Ss[h] = Ss[h] * jnp.exp(g_last) + _dot_tn(kg, v_new)
                v_news[n].append(v_new)
        for h in heads:
            s_scr[j, h] = Ss[h]
    for n, (b, h) in enumerate(items):
        hs = slice(h * HEAD_DIM, (h + 1) * HEAD_DIM)
        rb = slice(b * BR, (b + 1) * BR)
        v_new_all = v_news[n][0] if len(v_news[n]) == 1 else jnp.concatenate(v_news[n], axis=0)
        o_inter = o_inters[n][0] if len(o_inters[n]) == 1 else jnp.concatenate(o_inters[n], axis=0)
        o = o_inter + _dot(attns[n], v_new_all)
        o = o * lax.rsqrt(jnp.mean(o * o, axis=-1, keepdims=True) + EPS) * ng
        o_ref[rb, hs] = o * _silu(z_ref[rb, hs])

    @pl.when(c == nsteps - 1)
    def _():
        sout_ref[...] = s_scr[...]


def _dn_tiling(B, T):
    if T >= DN_ROWS:
        return 1, (DN_STEP_ROWS if T % DN_STEP_ROWS == 0 else DN_ROWS)
    return min(B, DN_ROWS // T), T


def _deltanet(proj, abT, conv0, S0, convw_l, alog_row, dtb_row, alog_col, dtb_col, ng, B, T):
    C = CHUNK if T % CHUNK == 0 else T
    ns, rs = _dn_tiling(B, T)
    assert T % rs == 0 and rs % C == 0 and B % ns == 0
    R = ns * rs
    nsteps = T // rs
    W3 = 3 * HD
    const = lambda b, c: (0, 0)
    return pl.pallas_call(
        functools.partial(_dn_kernel, ns=ns, rs=rs, C=C, nsteps=nsteps),
        grid=(B // ns, nsteps),
        in_specs=[
            pl.BlockSpec((R, W3), lambda b, c: (b * nsteps + c, C_DNQKV // W3)),
            pl.BlockSpec((R, HD), lambda b, c: (b * nsteps + c, C_DNZ // HD)),
            pl.BlockSpec((R, LANE), lambda b, c: (b * nsteps + c, C_AB // LANE)),
            pl.BlockSpec((8, R), lambda b, c: (0, b * nsteps + c)),
            pl.BlockSpec((ns, 8, W3), lambda b, c: (b, 0, 0)),
            pl.BlockSpec((ns, N_HEADS, HEAD_DIM, HEAD_DIM), lambda b, c: (b, 0, 0, 0)),
            pl.BlockSpec((CONV_W, W3), const),
            pl.BlockSpec((1, LANE), const),
            pl.BlockSpec((1, LANE), const),
            pl.BlockSpec((8, R), const),
            pl.BlockSpec((8, R), const),
            pl.BlockSpec((1, HEAD_DIM), const),
        ],
        out_specs=[
            pl.BlockSpec((R, HD), lambda b, c: (b * nsteps + c, 0)),
            pl.BlockSpec((ns, N_HEADS, HEAD_DIM, HEAD_DIM), lambda b, c: (b, 0, 0, 0)),
        ],
        out_shape=[
            jax.ShapeDtypeStruct((B * T, HD), F32),
            jax.ShapeDtypeStruct((B, N_HEADS, HEAD_DIM, HEAD_DIM), F32),
        ],
        scratch_shapes=[
            pltpu.VMEM((ns, 8 + rs, W3), F32),
            pltpu.VMEM((ns, N_HEADS, HEAD_DIM, HEAD_DIM), F32),
        ],
        compiler_params=_cparams(("parallel", "arbitrary")),
        name="deltanet",
    )(proj, proj, proj, abT, conv0, S0, convw_l, alog_row, dtb_row, alog_col, dtb_col, ng)


def _hg_refs(g, s, C):
    if s % 8 == 0:
        n = C // s
        first = jnp.broadcast_to(g.reshape(n, s, HD)[:, 0:1, :], (n, s, HD)).reshape(C, HD)
        return first, jnp.concatenate([first[s:, :], first[C - s:, :]], axis=0)
    assert s == 4
    n = C // 8
    g3 = g.reshape(n, 8, HD)
    r0 = jnp.broadcast_to(g3[:, 0:1, :], (n, 8, HD)).reshape(C, HD)
    r4 = jnp.broadcast_to(g3[:, 4:5, :], (n, 8, HD)).reshape(C, HD)
    sub = lax.broadcasted_iota(jnp.int32, (C, HD), 0) & 7
    return jnp.where(sub < 4, r0, r4), r4


def _hg_kernel(q_ref, f_ref, i_ref, z_ref, s0_ref, logit_ref, ng_ref, o_ref, sout_ref,
               kpad, gpad, vpad, st_scr, *, C, nc, layer):
    c = pl.program_id(1)
    sb = min(HG_SUB, C)

    @pl.when(c == 0)
    def _():
        for h in range(N_HEADS):
            st_scr[h] = s0_ref[h].T
        kpad[0:sb, :] = jnp.zeros((sb, HD), F32)
        gpad[0:sb, :] = jnp.zeros((sb, HD), F32)
        vpad[0:sb, :] = jnp.zeros((sb, HD), F32)

    logits = logit_ref[...]
    e = jnp.exp(logits - jnp.max(logits, axis=0, keepdims=True))
    p = e / jnp.sum(e, axis=0, keepdims=True)
    lb = jnp.sum(p[0:layer + 1, :], axis=0, keepdims=True) - p[0:1, :]

    q = _silu(q_ref[...]) * (HEAD_DIM ** -0.5)
    fg = lb + (1.0 - lb) * _sigmoid(f_ref[...])
    k = 1.0 - fg
    v = i_ref[...]
    lf = jnp.log(fg)
    row, col = _iota2((C, C))
    gcum = _dot_exact_lhs((row >= col).astype(BF16), lf)

    kpad[sb:sb + C, :] = k
    gpad[sb:sb + C, :] = gcum
    vpad[sb:sb + C, :] = v

    sd = min(HG_DIAG, sb)
    rpos = lax.broadcasted_iota(jnp.int32, (C, HD), 0) % sd
    o_diag = [jnp.zeros((C, HEAD_DIM), F32) for _ in range(N_HEADS)]
    for dl in range(sd):
        kd = kpad[sb - dl:sb - dl + C, :]
        gd = gpad[sb - dl:sb - dl + C, :]
        vd = vpad[sb - dl:sb - dl + C, :]
        ex = jnp.exp(jnp.where(rpos >= dl, gcum - gd, -jnp.inf))
        t = q * kd * ex
        for h in range(N_HEADS):
            hs = slice(h * HEAD_DIM, (h + 1) * HEAD_DIM)
            s = jnp.sum(t[:, hs], axis=-1, keepdims=True)
            o_diag[h] = o_diag[h] + s * vd[:, hs]

    eg = jnp.exp(gcum)
    g_last = gcum[C - 1:C, :]
    k_tail = k * jnp.exp(g_last - gcum)
    ng = ng_ref[...]
    nsb = C // sb
    heads = range(N_HEADS)
    hsl = [slice(h * HEAD_DIM, (h + 1) * HEAD_DIM) for h in heads]
    STs = [st_scr[h] for h in heads]
    os_ = [_dot_nt(q[:, hsl[h]] * eg[:, hsl[h]], STs[h]) + o_diag[h] for h in heads]
    for h in heads:
        st_scr[h] = STs[h] * jnp.exp(g_last[:, hsl[h]]) + _dot_tn(v[:, hsl[h]], k_tail[:, hsl[h]])
    attn = [jnp.zeros((C, C), F32) for _ in heads]
    s = sb // 2
    while s >= sd:
        gfirst, gnext = _hg_refs(gcum, s, C)
        qt = q * jnp.exp(gcum - gfirst)
        kt = k * jnp.exp(jnp.minimum(gnext - gcum, 0.0))
        sh = _log2(s)
        pair = (((row >> (sh + 1)) == (col >> (sh + 1)))
                & (((row >> sh) & 1) == 1) & (((col >> sh) & 1) == 0))
        attn = [attn[h] + jnp.where(pair, _dot_nt(qt[:, hsl[h]], kt[:, hsl[h]]), 0.0) for h in heads]
        s //= 2
    if nsb > 1:
        gblk = jnp.broadcast_to(gcum.reshape(nsb, sb, HD)[:, 0:1, :], (nsb, sb, HD)).reshape(C, HD)
        qt = q * jnp.exp(gcum - gblk)
        shb = _log2(sb)
        for I in range(1, nsb):
            gn = gcum[I * sb:I * sb + 1, :]
            kt = k * jnp.exp(jnp.minimum(gn - gcum, 0.0))
            sel = ((row >> shb) == I) & (col < I * sb)
            attn = [attn[h] + jnp.where(sel, _dot_nt(qt[:, hsl[h]], kt[:, hsl[h]]), 0.0) for h in heads]
    if nsb > 1 or sb > sd:
        os_ = [os_[h] + _dot(attn[h], v[:, hsl[h]]) for h in heads]
    for h in heads:
        o = os_[h]
        o = o * lax.rsqrt(jnp.mean(o * o, axis=-1, keepdims=True) + EPS) * ng
        o_ref[:, hsl[h]] = o * _silu(z_ref[:, hsl[h]])

    @pl.when(c == nc - 1)
    def _():
        for h in range(N_HEADS):
            sout_ref[h] = st_scr[h].T


def _hgrn2(proj, S0, logits, ng, l, B, T):
    C = CHUNK if T % CHUNK == 0 else T
    nc = T // C
    sb = min(HG_SUB, C)
    depth = logits.shape[0]
    const = lambda b, c: (0, 0)
    return pl.pallas_call(
        functools.partial(_hg_kernel, C=C, nc=nc, layer=l),
        grid=(B, nc),
        in_specs=[
            pl.BlockSpec((C, HD), lambda b, c: (b * nc + c, C_HGQ // HD)),
            pl.BlockSpec((C, HD), lambda b, c: (b * nc + c, C_HGF // HD)),
            pl.BlockSpec((C, HD), lambda b, c: (b * nc + c, C_HGI // HD)),
            pl.BlockSpec((C, HD), lambda b, c: (b * nc + c, C_HGZ // HD)),
            pl.BlockSpec((None, N_HEADS, HEAD_DIM, HEAD_DIM), lambda b, c: (b, 0, 0, 0)),
            pl.BlockSpec((depth, HD), const),
            pl.BlockSpec((1, HEAD_DIM), const),
        ],
        out_specs=[
            pl.BlockSpec((C, HD), lambda b, c: (b * nc + c, 0)),
            pl.BlockSpec((None, N_HEADS, HEAD_DIM, HEAD_DIM), lambda b, c: (b, 0, 0, 0)),
        ],
        out_shape=[
            jax.ShapeDtypeStruct((B * T, HD), F32),
            jax.ShapeDtypeStruct((B, N_HEADS, HEAD_DIM, HEAD_DIM), F32),
        ],
        scratch_shapes=[
            pltpu.VMEM((sb + C, HD), F32),
            pltpu.VMEM((sb + C, HD), F32),
            pltpu.VMEM((sb + C, HD), F32),
            pltpu.VMEM((N_HEADS, HEAD_DIM, HEAD_DIM), F32),
        ],
        compiler_params=_cparams(("parallel", "arbitrary")),
        name="hgrn2",
    )(proj, proj, proj, proj, S0, logits, ng)


def _merge_kernel(x_ref, oa_ref, ob_ref, oc_ref, ga_ref, gb_ref, gc_ref, wa_ref, wb_ref, wc_ref,
                  wo_ref, o_ref):
    mixed = (_sigmoid(ga_ref[...]) * _dot(oa_ref[...], wa_ref[...])
             + _sigmoid(gb_ref[...]) * _dot(ob_ref[...], wb_ref[...])
             + _sigmoid(gc_ref[...]) * _dot(oc_ref[...], wc_ref[...]))
    o_ref[...] = x_ref[...] + _dot(mixed, wo_ref[...])


def _merge(x2d, oa, ob, oc, proj, wa, wb, wc, wo, l, tm):
    M, D = x2d.shape
    row = lambda i: (i, 0)
    wspec = lambda shape: pl.BlockSpec((None,) + shape, lambda i: (l, 0, 0))
    g0 = C_GATE // D
    return pl.pallas_call(
        _merge_kernel,
        grid=(M // tm,),
        in_specs=[
            pl.BlockSpec((tm, D), row),
            pl.BlockSpec((tm, HD), row), pl.BlockSpec((tm, HD), row), pl.BlockSpec((tm, HD), row),
            pl.BlockSpec((tm, D), lambda i: (i, g0)),
            pl.BlockSpec((tm, D), lambda i: (i, g0 + 1)),
            pl.BlockSpec((tm, D), lambda i: (i, g0 + 2)),
            wspec((HD, D)), wspec((HD, D)), wspec((HD, D)), wspec((D, D)),
        ],
        out_specs=pl.BlockSpec((tm, D), row),
        out_shape=jax.ShapeDtypeStruct((M, D), F32),
        compiler_params=_cparams(("parallel",)),
        name="merge",
    )(x2d, oa, ob, oc, proj, proj, proj, wa, wb, wc, wo)


def _ffn_kernel(x_ref, g_ref, win_ref, wout_ref, gf_ref, o_ref, *, dff, tf, final):
    x = x_ref[...]
    h = (x * lax.rsqrt(jnp.mean(x * x, axis=-1, keepdims=True) + EPS) * g_ref[...]).astype(BF16)
    acc = x
    for j in range(dff // tf):
        gg = jnp.dot(h, win_ref[:, j * tf:(j + 1) * tf], preferred_element_type=F32)
        uu = jnp.dot(h, win_ref[:, dff + j * tf:dff + (j + 1) * tf], preferred_element_type=F32)
        act = (_silu(gg) * uu).astype(BF16)
        acc = acc + jnp.dot(act, wout_ref[j * tf:(j + 1) * tf, :], preferred_element_type=F32)
    if final:
        acc = acc * lax.rsqrt(jnp.mean(acc * acc, axis=-1, keepdims=True) + EPS) * gf_ref[...]
    o_ref[...] = acc


def _ffn(x2d, g_all, win, wout, gf, l, tm, final):
    M, D = x2d.shape
    dff = wout.shape[1]
    tf = 256
    return pl.pallas_call(
        functools.partial(_ffn_kernel, dff=dff, tf=tf, final=final),
        grid=(M // tm,),
        in_specs=[
            pl.BlockSpec((tm, D), lambda i: (i, 0)),
            pl.BlockSpec((None, 1, D), lambda i: (l, 0, 0)),
            pl.BlockSpec((None, D, 2 * dff), lambda i: (l, 0, 0)),
            pl.BlockSpec((None, dff, D), lambda i: (l, 0, 0)),
            pl.BlockSpec((1, D), lambda i: (0, 0)),
        ],
        out_specs=pl.BlockSpec((tm, D), lambda i: (i, 0)),
        out_shape=jax.ShapeDtypeStruct((M, D), F32),
        compiler_params=_cparams(("parallel",)),
        name="ffn",
    )(x2d, g_all, win, wout, gf)


def _row_tile(M, cap):
    t = min(cap, M)
    while M % t:
        t //= 2
    return t


def _prep_weights(w_in, w_br_a, w_br_b, w_br_c, w_o, w_ffn_in, w_ffn_out):
    depth, D, _ = w_in.shape
    o_dnqkv = 3 * HD
    o_a = o_dnqkv + 3 * HD
    o_z = o_a + 2 * N_HEADS
    o_hg = o_z + HD
    o_gate = o_hg + 4 * HD
    pad = N_PROJ - (C_AB + 2 * N_HEADS)
    w_perm = jnp.concatenate([
        w_in[:, :, HD:3 * HD],
        w_in[:, :, 0:HD],
        w_in[:, :, o_dnqkv:o_a],
        w_in[:, :, o_gate:o_gate + 3 * D],
        w_in[:, :, o_z:o_hg],
        w_in[:, :, o_hg:o_gate],
        w_in[:, :, o_a:o_z],
        jnp.zeros((depth, D, pad), w_in.dtype),
    ], axis=-1).astype(BF16)
    bf = lambda a: a.astype(BF16)
    return w_perm, bf(w_br_a), bf(w_br_b), bf(w_br_c), bf(w_o), bf(w_ffn_in), bf(w_ffn_out)


def _pad_lanes(v, n=LANE):
    return jnp.pad(v, (0, n - v.shape[0]))


def _trunk(x, P, page_table, cache_k, cache_v, dn_conv, dn_S, hg_S):
    B, T, D = x.shape
    depth = P['ln1_g'].shape[0]
    M = B * T
    x2d = x.reshape(M, D)
    tm_proj = _row_tile(M, 1024)
    tm_tok = _row_tile(M, 512)
    W3 = 3 * HD
    dn_ns, dn_rs = _dn_tiling(B, T)
    dn_r = dn_ns * dn_rs
    bufs, dSs, hSs = [], [], []
    kst = jnp.zeros((depth, M, HD), F32)
    vst = jnp.zeros((depth, M, HD), F32)
    for l in range(depth):
        proj, kst, vst = _inproj(x2d, P['ln1_g3'], P['w_perm'], kst, vst, l, tm_proj)
        qkv3 = proj[:, C_DNQKV:C_DNQKV + W3].reshape(B, T, W3)
        if page_table is None:
            o_a = _sb_prompt(proj, P['sb_bias'][l], B, T)
            conv0 = jnp.zeros((B, 8, W3), F32)
            Sb0 = jnp.zeros((B, N_HEADS, HEAD_DIM, HEAD_DIM), F32)
            Sc0 = Sb0
            buf_new = qkv3[:, T - (CONV_W - 1):, :]
        else:
            o_a = _sb_decode(proj, cache_k, cache_v, page_table, P['sb_bias'][l], l, B, T)
            conv0 = jnp.pad(dn_conv[l], ((0, 0), (8 - (CONV_W - 1), 0), (0, 0)))
            Sb0, Sc0 = dn_S[l], hg_S[l]
            buf_new = jnp.concatenate([dn_conv[l], qkv3], axis=1)[:, T:, :]
        abT = proj[:, C_AB:C_AB + 8].T
        alog = P['dn_A_log'][l]
        dtb = P['dn_dt_bias'][l]
        alog_row = _pad_lanes(alog)[None, :]
        dtb_row = _pad_lanes(dtb)[None, :]
        alog_col = jnp.broadcast_to(_pad_lanes(alog, 8)[:, None], (8, dn_r))
        dtb_col = jnp.broadcast_to(_pad_lanes(dtb, 8)[:, None], (8, dn_r))
        o_b, Sb = _deltanet(proj, abT, conv0, Sb0, P['dn_conv_w'][l], alog_row, dtb_row, alog_col,
                            dtb_col, P['dn_norm_g'][l][None, :], B, T)
        o_c, Sc = _hgrn2(proj, Sc0, P['hg_lb_logits'], P['hg_norm_g'][l][None, :], l, B, T)
        x2d = _merge(x2d, o_a, o_b, o_c, proj, P['w_br_a'], P['w_br_b'], P['w_br_c'], P['w_o'], l, tm_tok)
        x2d = _ffn(x2d, P['ln2_g3'], P['w_ffn_in'], P['w_ffn_out'], P['lnf_g'][None, :], l, tm_tok,
                   final=(l == depth - 1))
        bufs.append(buf_new)
        dSs.append(Sb)
        hSs.append(Sc)
    y = x2d.reshape(B, T, D)
    kv_shape = (depth, B, T, N_HEADS, HEAD_DIM)
    return (y, kst.reshape(kv_shape), vst.reshape(kv_shape), jnp.stack(bufs), jnp.stack(dSs),
            jnp.stack(hSs))


def kernel(x_prompt, x_sample, cache_sb_k, cache_sb_v, state_dn_conv, state_dn_S, state_hg_S, page_table,
           ln1_g, w_in, sb_bias, dn_conv_w, dn_A_log, dn_dt_bias, dn_norm_g, hg_lb_logits, hg_norm_g,
           w_br_a, w_br_b, w_br_c, w_o, ln2_g, w_ffn_in, w_ffn_out, lnf_g):
    w_perm, wa, wb, wc, wo, wfi, wfo = _prep_weights(w_in, w_br_a, w_br_b, w_br_c, w_o, w_ffn_in, w_ffn_out)
    P = dict(ln1_g3=ln1_g[:, None, :], ln2_g3=ln2_g[:, None, :], w_perm=w_perm, sb_bias=sb_bias,
             dn_conv_w=dn_conv_w, dn_A_log=dn_A_log, dn_dt_bias=dn_dt_bias, dn_norm_g=dn_norm_g,
             hg_lb_logits=hg_lb_logits.astype(F32), hg_norm_g=hg_norm_g, w_br_a=wa, w_br_b=wb, w_br_c=wc,
             w_o=wo, w_ffn_in=wfi, w_ffn_out=wfo, lnf_g=lnf_g, ln1_g=ln1_g)
    depth, n_pool = cache_sb_k.shape[:2]
    ck = cache_sb_k.reshape(depth, n_pool, PAGE_ROWS, HEAD_DIM)
    cv = cache_sb_v.reshape(depth, n_pool, PAGE_ROWS, HEAD_DIM)
    y_p, sbk_p, sbv_p, conv_p, dnS_p, hgS_p = _trunk(x_prompt, P, None, None, None, None, None, None)
    y_s, sbk_s, sbv_s, conv_s, dnS_s, hgS_s = _trunk(x_sample, P, page_table, ck, cv, state_dn_conv,
                                                     state_dn_S, state_hg_S)
    return (y_p, y_s, sbk_p, sbv_p, sbk_s, sbv_s, conv_p, conv_s, dnS_p, dnS_s, hgS_p, hgS_s)
```

```python
import functools
import math

import jax
import jax.numpy as jnp
from jax import lax
from jax.experimental import pallas as pl
from jax.experimental.pallas import tpu as pltpu

F32 = jnp.float32
BF16 = jnp.bfloat16
EPS = 1e-6
LOG2E = math.log2(math.e)

N_HEADS = 4
HEAD_DIM = 128
HD = N_HEADS * HEAD_DIM
CONV_W = 4
PAGE = 128
PAGE_ROWS = PAGE * N_HEADS
CHUNK = 64
SUB = 16
HG_SUB = 16
HG_DIAG = 4
SBP_STREAMS = 4
LANE = 128
DN_ROWS = 128
DN_STEP_ROWS = 256
VMEM_LIMIT = 56 * 1024 * 1024

C_SBK, C_SBV, C_SBQ = 0, 512, 1024
C_DNQKV = 1536
C_GATE = 3072
C_DNZ = 6144
C_HGQ, C_HGF, C_HGI, C_HGZ = 6656, 7168, 7680, 8192
C_AB = 8704
N_PROJ = 9216
PROJ_TN = 1536
DEC_PAGES = 16


def _sigmoid(x):
    return 1.0 / (1.0 + jnp.exp(-x))


def _silu(x):
    return x * _sigmoid(x)


def _softplus(x):
    return jnp.maximum(x, 0.0) + jnp.log(1.0 + jnp.exp(-jnp.abs(x)))


def _dot(a, b):
    return jnp.dot(a.astype(BF16), b.astype(BF16), preferred_element_type=F32)


def _dot_nt(a, b):
    return lax.dot_general(a.astype(BF16), b.astype(BF16), (((1,), (1,)), ((), ())),
                           preferred_element_type=F32)


def _dot_tn(a, b):
    return lax.dot_general(a.astype(BF16), b.astype(BF16), (((0,), (0,)), ((), ())),
                           preferred_element_type=F32)


def _split(a, n):
    out = []
    r = a
    for t in range(n):
        p = r.astype(BF16)
        out.append(p)
        if t + 1 < n:
            r = r - p.astype(F32)
    return out


def _split_trunc(a):
    bits = lax.bitcast_convert_type(a, jnp.uint32) & jnp.uint32(0xFFFF0000)
    hi = lax.bitcast_convert_type(bits, F32)
    return hi.astype(BF16), (a - hi).astype(BF16)


def _dot_exact_rhs(a, m_bf16, n=3):
    acc = None
    for p in _split(a, n):
        t = jnp.dot(p, m_bf16, preferred_element_type=F32)
        acc = t if acc is None else acc + t
    return acc


def _dot_exact_lhs(m_bf16, a, n=3):
    acc = None
    for p in _split(a, n):
        t = jnp.dot(m_bf16, p, preferred_element_type=F32)
        acc = t if acc is None else acc + t
    return acc


def _dot_hp(a, b, nt=False):
    a1, a2 = _split(a, 2)
    b1, b2 = _split(b, 2)
    if nt:
        f = lambda x, y: lax.dot_general(x, y, (((1,), (1,)), ((), ())), preferred_element_type=F32)
    else:
        f = lambda x, y: jnp.dot(x, y, preferred_element_type=F32)
    return f(a1, b1) + (f(a1, b2) + f(a2, b1))


def _iota2(shape):
    return (lax.broadcasted_iota(jnp.int32, shape, 0), lax.broadcasted_iota(jnp.int32, shape, 1))


def _log2(n):
    k = n.bit_length() - 1
    assert (1 << k) == n
    return k


def _cparams(sem):
    return pltpu.CompilerParams(dimension_semantics=sem, vmem_limit_bytes=VMEM_LIMIT)


def _inproj_kernel(x_ref, g_ref, w_ref, kin_ref, vin_ref, o_ref, kst_ref, vst_ref, h_scr):
    del kin_ref, vin_ref
    j = pl.program_id(1)

    @pl.when(j == 0)
    def _():
        x = x_ref[...]
        ms = jnp.mean(x * x, axis=-1, keepdims=True)
        h_scr[...] = (x * lax.rsqrt(ms + EPS) * g_ref[...]).astype(BF16)

    res = jnp.dot(h_scr[...], w_ref[...], preferred_element_type=F32)
    o_ref[...] = res

    @pl.when(j == 0)
    def _():
        kst_ref[...] = res[:, C_SBK:C_SBK + HD]
        vst_ref[...] = res[:, C_SBV:C_SBV + HD]


def _inproj(x2d, g_all, w_all, kst, vst, l, tm):
    M, D = x2d.shape
    depth = kst.shape[0]
    kv_spec = pl.BlockSpec((None, tm, HD), lambda i, j: (l, i, 0))
    return pl.pallas_call(
        _inproj_kernel,
        grid=(M // tm, N_PROJ // PROJ_TN),
        in_specs=[
            pl.BlockSpec((tm, D), lambda i, j: (i, 0)),
            pl.BlockSpec((None, 1, D), lambda i, j: (l, 0, 0)),
            pl.BlockSpec((None, D, PROJ_TN), lambda i, j: (l, 0, j)),
            pl.BlockSpec(memory_space=pl.ANY),
            pl.BlockSpec(memory_space=pl.ANY),
        ],
        out_specs=[pl.BlockSpec((tm, PROJ_TN), lambda i, j: (i, j)), kv_spec, kv_spec],
        out_shape=[jax.ShapeDtypeStruct((M, N_PROJ), F32),
                   jax.ShapeDtypeStruct((depth, M, HD), F32),
                   jax.ShapeDtypeStruct((depth, M, HD), F32)],
        input_output_aliases={3: 1, 4: 2},
        scratch_shapes=[pltpu.VMEM((tm, D), BF16)],
        compiler_params=_cparams(("parallel", "arbitrary")),
        name="inproj",
    )(x2d, g_all, w_all, kst, vst)


def _log2_sig_pair(z2):
    lg = jnp.log2(1.0 + jnp.exp2(-jnp.abs(z2)))
    ls = jnp.minimum(z2, 0.0) - lg
    return ls, ls - z2


def _sbp_blocks(qs, k, v, bias2, U, acc, csum, streams, valids):
    z2 = [_dot_nt(qs[s], k) + bias2 for s in streams]
    pairs = [_log2_sig_pair(z) for z in z2]
    l1ms = [p[1] if m is None else jnp.where(m, p[1], 0.0) for p, m in zip(pairs, valids)]
    splits = [_split_trunc(l) for l in l1ms]
    sufs = [jnp.dot(jnp.concatenate([hi, lo], axis=1), U, preferred_element_type=F32) for hi, lo in splits]
    ws = [jnp.exp2(p[0] + suf + csum[s]) for p, suf, s in zip(pairs, sufs, streams)]
    ws = [w if m is None else jnp.where(m, w, 0.0) for w, m in zip(ws, valids)]
    for w, l, s in zip(ws, l1ms, streams):
        acc[s] += _dot(w, v)
        csum[s] += jnp.sum(l, axis=-1, keepdims=True)


def _sbp_kernel(bias_ref, q_ref, k_ref, v_ref, u_ref, o_ref, kbf, vbf, acc, csum, *, tq, nstr, layer):
    h = pl.program_id(1)
    i = pl.program_id(2)

    @pl.when(i == 0)
    def _():
        kbf[...] = k_ref[...].astype(BF16)
        vbf[...] = v_ref[...].astype(BF16)

    bias2 = bias_ref[layer, h] * LOG2E
    q_all = (q_ref[...] * (HEAD_DIM ** -0.5 * LOG2E)).astype(BF16)
    qs = [q_all[s * tq:(s + 1) * tq, :] for s in range(nstr)]
    U = u_ref[...]
    acc[...] = jnp.zeros_like(acc)
    csum[...] = jnp.zeros_like(csum)

    row, col = _iota2((tq, tq))
    causal = col < row

    def kv(kb):
        st = pl.multiple_of(kb * tq, tq)
        return kbf[pl.ds(st, tq), :], vbf[pl.ds(st, tq), :]

    for t in range(nstr):
        d = nstr - 1 - t
        k, v = kv(nstr * i + d)
        streams = list(range(d, nstr))
        _sbp_blocks(qs, k, v, bias2, U, acc, csum, streams, [causal] + [None] * (len(streams) - 1))

    def body(jj, carry):
        k, v = kv(nstr * i - 1 - jj)
        _sbp_blocks(qs, k, v, bias2, U, acc, csum, list(range(nstr)), [None] * nstr)
        return carry

    lax.fori_loop(0, nstr * i, body, 0)
    for s in range(nstr):
        o_ref[s * tq:(s + 1) * tq, :] = acc[s]


def _sb_prompt(proj, bias_all, l, B, T):
    tq = min(256, T)
    nstr = SBP_STREAMS
    while T % (nstr * tq):
        nstr //= 2
    tqs = nstr * tq
    nq = T // tqs
    r = jnp.arange(tq)
    U1 = (r[:, None] > r[None, :]).astype(BF16)
    U = jnp.concatenate([U1, U1], axis=0)
    return pl.pallas_call(
        functools.partial(_sbp_kernel, tq=tq, nstr=nstr, layer=l),
        grid=(B, N_HEADS, nq),
        in_specs=[
            pl.BlockSpec(memory_space=pltpu.SMEM),
            pl.BlockSpec((tqs, HEAD_DIM), lambda b, h, i: (b * nq + i, C_SBQ // HEAD_DIM + h)),
            pl.BlockSpec((T, HEAD_DIM), lambda b, h, i: (b, C_SBK // HEAD_DIM + h)),
            pl.BlockSpec((T, HEAD_DIM), lambda b, h, i: (b, C_SBV // HEAD_DIM + h)),
            pl.BlockSpec((2 * tq, tq), lambda b, h, i: (0, 0)),
        ],
        out_specs=pl.BlockSpec((tqs, HEAD_DIM), lambda b, h, i: (b * nq + i, h)),
        out_shape=jax.ShapeDtypeStruct((B * T, HD), F32),
        scratch_shapes=[
            pltpu.VMEM((T, HEAD_DIM), BF16),
            pltpu.VMEM((T, HEAD_DIM), BF16),
            pltpu.VMEM((nstr, tq, HEAD_DIM), F32),
            pltpu.VMEM((nstr, tq, 1), F32),
        ],
        compiler_params=_cparams(("parallel", "parallel", "arbitrary")),
        name="sb_prompt",
    )(bias_all, proj, proj, proj, U)


def _sbd_group(qr, ks, vs, bias2, U, hm, extra_valid, acc_ref, csum_ref):
    n = len(ks)
    nstr = min(2, n)
    per = n // nstr
    rows = qr.shape[0]
    cat = lambda xs, ax: xs[0] if len(xs) == 1 else jnp.concatenate(xs, axis=ax)
    valid = hm if extra_valid is None else extra_valid
    valid_all = cat([valid] * per, 1)
    groups = [list(range(s * per, (s + 1) * per)) for s in range(nstr)]
    z2s = [_dot_nt(qr, cat([ks[p] for p in g], 0)) + bias2 for g in groups]
    pairs = [_log2_sig_pair(z) for z in z2s]
    l1ms = [jnp.where(valid_all, p[1], 0.0) for p in pairs]
    pages = [[l[:, p * PAGE_ROWS:(p + 1) * PAGE_ROWS] for p in range(per)] for l in l1ms]
    c = csum_ref[...]
    carries = []
    for s in range(nstr):
        cs = []
        for p in range(per):
            cs.append(c)
            c = c + jnp.sum(pages[s][p], axis=-1, keepdims=True)
        carries.append(cs)
    csum_ref[...] = c
    splits = [_split(cat(pg, 0), 2) for pg in pages]
    sufs = [jnp.dot(hi, U, preferred_element_type=F32) + jnp.dot(lo, U, preferred_element_type=F32)
            for hi, lo in splits]
    out = None
    for s in range(nstr):
        ws = []
        for p in range(per):
            e = (pairs[s][0][:, p * PAGE_ROWS:(p + 1) * PAGE_ROWS] + sufs[s][p * rows:(p + 1) * rows, :]
                 + carries[s][p])
            ws.append(jnp.where(valid, jnp.exp2(e), 0.0).astype(BF16))
        t = jnp.dot(cat(ws, 1), cat([vs[p] for p in groups[s]], 0), preferred_element_type=F32)
        out = t if out is None else out + t
    acc_ref[...] += out


def _sbd_kernel(pt_ref, bias_ref, q_ref, kn_ref, vn_ref, *rest, G, n_groups, ts):
    kpages = rest[:G]
    vpages = rest[G:2 * G]
    u_ref, o_ref, qr, kbuf, vbuf, acc, csum = rest[2 * G:]
    del pt_ref
    g = pl.program_id(1)
    rows = N_HEADS * ts
    U = u_ref[...]
    bias2 = bias_ref[:, 0:1] * LOG2E
    row, col = _iota2((rows, PAGE_ROWS))
    hm = (col & (N_HEADS - 1)) == (row // ts)

    @pl.when(g == 0)
    def _():
        acc[...] = jnp.zeros_like(acc)
        csum[...] = jnp.zeros_like(csum)
        q = q_ref[...] * (HEAD_DIM ** -0.5 * LOG2E)
        for hh in range(N_HEADS):
            qr[hh * ts:(hh + 1) * ts, :] = q[:, hh * HEAD_DIM:(hh + 1) * HEAD_DIM].astype(BF16)
        kbuf[...] = jnp.zeros_like(kbuf)
        vbuf[...] = jnp.zeros_like(vbuf)
        kbuf[0:ts * N_HEADS, :] = kn_ref[...].astype(BF16)
        vbuf[0:ts * N_HEADS, :] = vn_ref[...].astype(BF16)
        valid_new = hm & ((col >> 2) < (row % ts))
        _sbd_group(qr[...], [kbuf[...]], [vbuf[...]], bias2, U, hm, valid_new, acc, csum)

    @pl.when(g > 0)
    def _():
        _sbd_group(qr[...], [r[...].astype(BF16) for r in kpages], [r[...].astype(BF16) for r in vpages],
                   bias2, U, hm, None, acc, csum)

    @pl.when(g == n_groups)
    def _():
        for hh in range(N_HEADS):
            o_ref[:, hh * HEAD_DIM:(hh + 1) * HEAD_DIM] = acc[hh * ts:(hh + 1) * ts, :]


def _sb_decode(proj, cache_k, cache_v, page_table, bias_rows, l, nb, ts):
    npg = page_table.shape[1]
    G = min(DEC_PAGES, npg)
    n_groups = npg // G
    rows = N_HEADS * ts
    r = jnp.arange(PAGE_ROWS)
    U = (((r[:, None] & 3) == (r[None, :] & 3)) & (r[:, None] > r[None, :])).astype(BF16)
    kn = proj[:, C_SBK:C_SBK + HD].reshape(nb * ts * N_HEADS, HEAD_DIM)
    vn = proj[:, C_SBV:C_SBV + HD].reshape(nb * ts * N_HEADS, HEAD_DIM)

    def page_spec(p):
        def imap(b, g, pt):
            first = jnp.maximum(g - 1, 0) * G
            return (l, pt[b, npg - 1 - (first + p)], 0, 0)
        return pl.BlockSpec((None, None, PAGE_ROWS, HEAD_DIM), imap)

    in_specs = [
        pl.BlockSpec((None, rows, LANE), lambda b, g, pt: (l, 0, 0)),
        pl.BlockSpec((ts, HD), lambda b, g, pt: (b, C_SBQ // HD)),
        pl.BlockSpec((rows, HEAD_DIM), lambda b, g, pt: (b, 0)),
        pl.BlockSpec((rows, HEAD_DIM), lambda b, g, pt: (b, 0)),
    ] + [page_spec(p) for p in range(G)] + [page_spec(p) for p in range(G)] + [
        pl.BlockSpec((PAGE_ROWS, PAGE_ROWS), lambda b, g, pt: (0, 0)),
    ]
    grid_spec = pltpu.PrefetchScalarGridSpec(
        num_scalar_prefetch=1,
        grid=(nb, n_groups + 1),
        in_specs=in_specs,
        out_specs=pl.BlockSpec((ts, HD), lambda b, g, pt: (b, 0)),
        scratch_shapes=[
            pltpu.VMEM((rows, HEAD_DIM), BF16),
            pltpu.VMEM((PAGE_ROWS, HEAD_DIM), BF16),
            pltpu.VMEM((PAGE_ROWS, HEAD_DIM), BF16),
            pltpu.VMEM((rows, HEAD_DIM), F32),
            pltpu.VMEM((rows, 1), F32),
        ],
    )
    return pl.pallas_call(
        functools.partial(_sbd_kernel, G=G, n_groups=n_groups, ts=ts),
        grid_spec=grid_spec,
        out_shape=jax.ShapeDtypeStruct((nb * ts, HD), F32),
        compiler_params=_cparams(("parallel", "arbitrary")),
        name="sb_decode",
    )(page_table, bias_rows, proj, kn, vn, *([cache_k] * G), *([cache_v] * G), U)


def _unit_lower_inverse(As, C, row, col):
    eye = (row == col).astype(F32)
    blk = min(SUB, C)
    sh = _log2(blk)
    same = (row >> sh) == (col >> sh)
    Ms = [-jnp.where(same, A, 0.0) for A in As]
    Rs = [eye + M for M in Ms]
    for _ in range(sh - 1):
        Ms = [_dot_hp(M, M) for M in Ms]
        Rs = [R + _dot_hp(R, M) for R, M in zip(Rs, Ms)]
    if C > blk:
        Ps = [-_dot_hp(R, jnp.where(same, 0.0, A)) for R, A in zip(Rs, As)]
        Xs = [eye + P for P in Ps]
        for _ in range(_log2(C // blk) - 1):
            Ps = [_dot_hp(P, P) for P in Ps]
            Xs = [X + _dot_hp(X, P) for X, P in zip(Xs, Ps)]
        Rs = [_dot_hp(X, R) for X, R in zip(Xs, Rs)]
    return Rs


def _dn_kernel(qkv_ref, z_ref, ab_ref, abT_ref, conv0_ref, s0_ref, convw_ref, alog_row_ref,
               dtb_row_ref, alog_col_ref, dtb_col_ref, ng_ref, o_ref, sout_ref, xbuf, s_scr,
               *, ns, rs, C, nsteps):
    c = pl.program_id(1)
    R = ns * rs
    npc = rs // C

    @pl.when(c == 0)
    def _():
        xbuf[:, 0:8, :] = conv0_ref[...]
        s_scr[...] = s0_ref[...]

    @pl.when(c > 0)
    def _():
        xbuf[:, 0:8, :] = xbuf[:, rs:rs + 8, :]

    ys = []
    for j in range(ns):
        xbuf[j, 8:8 + rs, :] = qkv_ref[j * rs:(j + 1) * rs, :]
        acc = xbuf[j, 5:5 + rs, :] * convw_ref[0:1, :]
        for t in range(1, CONV_W):
            acc = acc + xbuf[j, 5 + t:5 + t + rs, :] * convw_ref[t:t + 1, :]
        ys.append(acc)
    y = _silu(ys[0] if ns == 1 else jnp.concatenate(ys, axis=0))

    BR = min(R, DN_ROWS)
    nblk = R // BR
    row, col = _iota2((BR, BR))
    shc = _log2(C)
    same_chunk = (row >> shc) == (col >> shc)
    lower_incl = same_chunk & (row >= col)
    tri_lower = lower_incl.astype(BF16)
    tri_upper = (same_chunk & (row <= col)).astype(BF16)
    heads = range(N_HEADS)
    items = [(b, h) for b in range(nblk) for h in heads]
    it = {bh: n for n, bh in enumerate(items)}

    beta_blk, gcol_blk, grow_blk = [], [], []
    for b in range(nblk):
        ab = ab_ref[b * BR:(b + 1) * BR, :]
        g_col = -jnp.exp(alog_row_ref[...]) * _softplus(ab + dtb_row_ref[...])
        beta_blk.append(_sigmoid(ab))
        gcol_blk.append(_dot_exact_lhs(tri_lower, g_col))
        abT = abT_ref[:, b * BR:(b + 1) * BR]
        g_row = -jnp.exp(alog_col_ref[:, 0:BR]) * _softplus(abT + dtb_col_ref[:, 0:BR])
        grow_blk.append(_dot_exact_rhs(g_row, tri_upper))

    ng = ng_ref[...]
    qs, kls, vs, betas, gcs, decays, kbs = [], [], [], [], [], [], []
    for b, h in items:
        yb = y[b * BR:(b + 1) * BR, :]
        q = yb[:, h * HEAD_DIM:(h + 1) * HEAD_DIM]
        k = yb[:, HD + h * HEAD_DIM:HD + (h + 1) * HEAD_DIM]
        vs.append(yb[:, 2 * HD + h * HEAD_DIM:2 * HD + (h + 1) * HEAD_DIM])
        qs.append(q * lax.rsqrt(jnp.sum(q * q, axis=-1, keepdims=True) + EPS) * (HEAD_DIM ** -0.5))
        kls.append(k * lax.rsqrt(jnp.sum(k * k, axis=-1, keepdims=True) + EPS))
        betas.append(beta_blk[b][:, N_HEADS + h:N_HEADS + h + 1])
        gcs.append(gcol_blk[b][:, h:h + 1])
        gr = grow_blk[b][h:h + 1, :]
        decays.append(jnp.exp(jnp.where(lower_incl, gcs[-1] - gr, -jnp.inf)))
        kbs.append(kls[-1] * betas[-1])
    nit = range(len(items))
    As = [jnp.where(row > col, _dot_hp(kbs[n], kls[n], nt=True) * decays[n], 0.0) for n in nit]
    Tms = _unit_lower_inverse(As, C, row, col)
    egs = [jnp.exp(gcs[n]) for n in nit]
    sols = [_dot_hp(Tms[n], jnp.concatenate([vs[n] * betas[n], kbs[n] * egs[n]], axis=-1)) for n in nit]
    us = [s[:, :HEAD_DIM] for s in sols]
    ws = [s[:, HEAD_DIM:] for s in sols]
    attns = [(_dot_nt(qs[n], kls[n]) * decays[n]).astype(BF16) for n in nit]
    qgs = [qs[n] * egs[n] for n in nit]
    v_news = [[] for _ in nit]
    o_inters = [[] for _ in nit]
    for j in range(ns):
        Ss = [s_scr[j, h] for h in heads]
        for i in range(npc):
            g0 = j * rs + i * C
            b, r0 = g0 // BR, g0 % BR
            rows = slice(r0, r0 + C)
            for h in heads:
                n = it[(b, h)]
                v_new = us[n][rows, :] - _dot(ws[n][rows, :], Ss[h])
                o_inters[n].append(_dot(qgs[n][rows, :], Ss[h]))
                g_last = gcs[n][r0 + C - 1:r0 + C, :]
                kg = kls[n][rows, :] * jnp.exp(g_last - gcs[n][rows, :])
                Ss[h] = Ss[h] * jnp.exp(g_last) + _dot_tn(kg, v_new)
                v_news[n].append(v_new)
        for h in heads:
            s_scr[j, h] = Ss[h]
    for n, (b, h) in enumerate(items):
        hs = slice(h * HEAD_DIM, (h + 1) * HEAD_DIM)
        rb = slice(b * BR, (b + 1) * BR)
        v_new_all = v_news[n][0] if len(v_news[n]) == 1 else jnp.concatenate(v_news[n], axis=0)
        o_inter = o_inters[n][0] if len(o_inters[n]) == 1 else jnp.concatenate(o_inters[n], axis=0)
        o = o_inter + _dot(attns[n], v_new_all)
        o = o * lax.rsqrt(jnp.mean(o * o, axis=-1, keepdims=True) + EPS) * ng
        o_ref[rb, hs] = o * _silu(z_ref[rb, hs])

    @pl.when(c == nsteps - 1)
    def _():
        sout_ref[...] = s_scr[...]


def _dn_tiling(B, T):
    if T >= DN_ROWS:
        return 1, (DN_STEP_ROWS if T % DN_STEP_ROWS == 0 else DN_ROWS)
    return min(B, DN_ROWS // T), T


def _deltanet(proj, abT, conv0, S0, convw, alog_row, dtb_row, alog_col, dtb_col, ng, l, ls, B, T):
    C = CHUNK if T % CHUNK == 0 else T
    ns, rs = _dn_tiling(B, T)
    assert T % rs == 0 and rs % C == 0 and B % ns == 0
    layer = lambda b, c: (l, 0, 0)
    R = ns * rs
    nsteps = T // rs
    W3 = 3 * HD
    const = lambda b, c: (0, 0)
    return pl.pallas_call(
        functools.partial(_dn_kernel, ns=ns, rs=rs, C=C, nsteps=nsteps),
        grid=(B // ns, nsteps),
        in_specs=[
            pl.BlockSpec((R, W3), lambda b, c: (b * nsteps + c, C_DNQKV // W3)),
            pl.BlockSpec((R, HD), lambda b, c: (b * nsteps + c, C_DNZ // HD)),
            pl.BlockSpec((R, LANE), lambda b, c: (b * nsteps + c, C_AB // LANE)),
            pl.BlockSpec((8, R), lambda b, c: (0, b * nsteps + c)),
            pl.BlockSpec((None, ns, 8, W3), lambda b, c: (ls, b, 0, 0)),
            pl.BlockSpec((None, ns, N_HEADS, HEAD_DIM, HEAD_DIM), lambda b, c: (ls, b, 0, 0, 0)),
            pl.BlockSpec((None, CONV_W, W3), layer),
            pl.BlockSpec((None, 1, LANE), layer),
            pl.BlockSpec((None, 1, LANE), layer),
            pl.BlockSpec((None, 8, R), layer),
            pl.BlockSpec((None, 8, R), layer),
            pl.BlockSpec((None, 1, HEAD_DIM), layer),
        ],
        out_specs=[
            pl.BlockSpec((R, HD), lambda b, c: (b * nsteps + c, 0)),
            pl.BlockSpec((ns, N_HEADS, HEAD_DIM, HEAD_DIM), lambda b, c: (b, 0, 0, 0)),
        ],
        out_shape=[
            jax.ShapeDtypeStruct((B * T, HD), F32),
            jax.ShapeDtypeStruct((B, N_HEADS, HEAD_DIM, HEAD_DIM), F32),
        ],
        scratch_shapes=[
            pltpu.VMEM((ns, 8 + rs, W3), F32),
            pltpu.VMEM((ns, N_HEADS, HEAD_DIM, HEAD_DIM), F32),
        ],
        compiler_params=_cparams(("parallel", "arbitrary")),
        name="deltanet",
    )(proj, proj, proj, abT, conv0, S0, convw, alog_row, dtb_row, alog_col, dtb_col, ng)


def _hg_refs(g, s, C):
    if s % 8 == 0:
        n = C // s
        first = jnp.broadcast_to(g.reshape(n, s, HD)[:, 0:1, :], (n, s, HD)).reshape(C, HD)
        return first, jnp.concatenate([first[s:, :], first[C - s:, :]], axis=0)
    assert s == 4
    n = C // 8
    g3 = g.reshape(n, 8, HD)
    r0 = jnp.broadcast_to(g3[:, 0:1, :], (n, 8, HD)).reshape(C, HD)
    r4 = jnp.broadcast_to(g3[:, 4:5, :], (n, 8, HD)).reshape(C, HD)
    sub = lax.broadcasted_iota(jnp.int32, (C, HD), 0) & 7
    return jnp.where(sub < 4, r0, r4), r4


def _hg_kernel(q_ref, f_ref, i_ref, z_ref, s0_ref, logit_ref, ng_ref, o_ref, sout_ref,
               kpad, gpad, vpad, st_scr, *, C, nc, layer):
    c = pl.program_id(1)
    sb = min(HG_SUB, C)

    @pl.when(c == 0)
    def _():
        for h in range(N_HEADS):
            st_scr[h] = s0_ref[h].T
        kpad[0:sb, :] = jnp.zeros((sb, HD), F32)
        gpad[0:sb, :] = jnp.zeros((sb, HD), F32)
        vpad[0:sb, :] = jnp.zeros((sb, HD), F32)

    logits = logit_ref[...]
    e = jnp.exp(logits - jnp.max(logits, axis=0, keepdims=True))
    p = e / jnp.sum(e, axis=0, keepdims=True)
    lb = jnp.sum(p[0:layer + 1, :], axis=0, keepdims=True) - p[0:1, :]

    q = _silu(q_ref[...]) * (HEAD_DIM ** -0.5)
    fg = lb + (1.0 - lb) * _sigmoid(f_ref[...])
    k = 1.0 - fg
    v = i_ref[...]
    lf = jnp.log(fg)
    row, col = _iota2((C, C))
    gcum = _dot_exact_lhs((row >= col).astype(BF16), lf)

    kpad[sb:sb + C, :] = k
    gpad[sb:sb + C, :] = gcum
    vpad[sb:sb + C, :] = v

    sd = min(HG_DIAG, sb) if C > 8 else sb
    rpos = lax.broadcasted_iota(jnp.int32, (C, HD), 0) % sd
    o_diag = [jnp.zeros((C, HEAD_DIM), F32) for _ in range(N_HEADS)]
    for dl in range(sd):
        kd = kpad[sb - dl:sb - dl + C, :]
        gd = gpad[sb - dl:sb - dl + C, :]
        vd = vpad[sb - dl:sb - dl + C, :]
        ex = jnp.exp(jnp.where(rpos >= dl, gcum - gd, -jnp.inf))
        t = q * kd * ex
        for h in range(N_HEADS):
            hs = slice(h * HEAD_DIM, (h + 1) * HEAD_DIM)
            s = jnp.sum(t[:, hs], axis=-1, keepdims=True)
            o_diag[h] = o_diag[h] + s * vd[:, hs]

    eg = jnp.exp(gcum)
    g_last = gcum[C - 1:C, :]
    k_tail = k * jnp.exp(g_last - gcum)
    ng = ng_ref[...]
    nsb = C // sb
    heads = range(N_HEADS)
    hsl = [slice(h * HEAD_DIM, (h + 1) * HEAD_DIM) for h in heads]
    STs = [st_scr[h] for h in heads]
    os_ = [_dot_nt(q[:, hsl[h]] * eg[:, hsl[h]], STs[h]) + o_diag[h] for h in heads]
    for h in heads:
        st_scr[h] = STs[h] * jnp.exp(g_last[:, hsl[h]]) + _dot_tn(v[:, hsl[h]], k_tail[:, hsl[h]])
    attn = [jnp.zeros((C, C), F32) for _ in heads]
    s = sb // 2
    while s >= sd:
        gfirst, gnext = _hg_refs(gcum, s, C)
        qt = q * jnp.exp(gcum - gfirst)
        kt = k * jnp.exp(jnp.minimum(gnext - gcum, 0.0))
        sh = _log2(s)
        pair = (((row >> (sh + 1)) == (col >> (sh + 1)))
                & (((row >> sh) & 1) == 1) & (((col >> sh) & 1) == 0))
        attn = [attn[h] + jnp.where(pair, _dot_nt(qt[:, hsl[h]], kt[:, hsl[h]]), 0.0) for h in heads]
        s //= 2
    if nsb > 1:
        gblk = jnp.broadcast_to(gcum.reshape(nsb, sb, HD)[:, 0:1, :], (nsb, sb, HD)).reshape(C, HD)
        qt = q * jnp.exp(gcum - gblk)
        shb = _log2(sb)
        for I in range(1, nsb):
            gn = gcum[I * sb:I * sb + 1, :]
            kt = k * jnp.exp(jnp.minimum(gn - gcum, 0.0))
            sel = ((row >> shb) == I) & (col < I * sb)
            attn = [attn[h] + jnp.where(sel, _dot_nt(qt[:, hsl[h]], kt[:, hsl[h]]), 0.0) for h in heads]
    if nsb > 1 or sb > sd:
        os_ = [os_[h] + _dot(attn[h], v[:, hsl[h]]) for h in heads]
    for h in heads:
        o = os_[h]
        o = o * lax.rsqrt(jnp.mean(o * o, axis=-1, keepdims=True) + EPS) * ng
        o_ref[:, hsl[h]] = o * _silu(z_ref[:, hsl[h]])

    @pl.when(c == nc - 1)
    def _():
        for h in range(N_HEADS):
            sout_ref[h] = st_scr[h].T


def _hgrn2(proj, S0, logits, ng, l, ls, B, T):
    C = CHUNK if T % CHUNK == 0 else T
    nc = T // C
    sb = min(HG_SUB, C)
    depth = logits.shape[0]
    const = lambda b, c: (0, 0)
    return pl.pallas_call(
        functools.partial(_hg_kernel, C=C, nc=nc, layer=l),
        grid=(B, nc),
        in_specs=[
            pl.BlockSpec((C, HD), lambda b, c: (b * nc + c, C_HGQ // HD)),
            pl.BlockSpec((C, HD), lambda b, c: (b * nc + c, C_HGF // HD)),
            pl.BlockSpec((C, HD), lambda b, c: (b * nc + c, C_HGI // HD)),
            pl.BlockSpec((C, HD), lambda b, c: (b * nc + c, C_HGZ // HD)),
            pl.BlockSpec((None, None, N_HEADS, HEAD_DIM, HEAD_DIM), lambda b, c: (ls, b, 0, 0, 0)),
            pl.BlockSpec((depth, HD), const),
            pl.BlockSpec((None, 1, HEAD_DIM), lambda b, c: (l, 0, 0)),
        ],
        out_specs=[
            pl.BlockSpec((C, HD), lambda b, c: (b * nc + c, 0)),
            pl.BlockSpec((None, N_HEADS, HEAD_DIM, HEAD_DIM), lambda b, c: (b, 0, 0, 0)),
        ],
        out_shape=[
            jax.ShapeDtypeStruct((B * T, HD), F32),
            jax.ShapeDtypeStruct((B, N_HEADS, HEAD_DIM, HEAD_DIM), F32),
        ],
        scratch_shapes=[
            pltpu.VMEM((sb + C, HD), F32),
            pltpu.VMEM((sb + C, HD), F32),
            pltpu.VMEM((sb + C, HD), F32),
            pltpu.VMEM((N_HEADS, HEAD_DIM, HEAD_DIM), F32),
        ],
        compiler_params=_cparams(("parallel", "arbitrary")),
        name="hgrn2",
    )(proj, proj, proj, proj, S0, logits, ng)


def _merge_kernel(x_ref, oa_ref, ob_ref, oc_ref, ga_ref, gb_ref, gc_ref, wa_ref, wb_ref, wc_ref,
                  wo_ref, o_ref):
    mixed = (_sigmoid(ga_ref[...]) * _dot(oa_ref[...], wa_ref[...])
             + _sigmoid(gb_ref[...]) * _dot(ob_ref[...], wb_ref[...])
             + _sigmoid(gc_ref[...]) * _dot(oc_ref[...], wc_ref[...]))
    o_ref[...] = x_ref[...] + _dot(mixed, wo_ref[...])


def _merge(x2d, oa, ob, oc, proj, wa, wb, wc, wo, l, tm):
    M, D = x2d.shape
    row = lambda i: (i, 0)
    wspec = lambda shape: pl.BlockSpec((None,) + shape, lambda i: (l, 0, 0))
    g0 = C_GATE // D
    return pl.pallas_call(
        _merge_kernel,
        grid=(M // tm,),
        in_specs=[
            pl.BlockSpec((tm, D), row),
            pl.BlockSpec((tm, HD), row), pl.BlockSpec((tm, HD), row), pl.BlockSpec((tm, HD), row),
            pl.BlockSpec((tm, D), lambda i: (i, g0)),
            pl.BlockSpec((tm, D), lambda i: (i, g0 + 1)),
            pl.BlockSpec((tm, D), lambda i: (i, g0 + 2)),
            wspec((HD, D)), wspec((HD, D)), wspec((HD, D)), wspec((D, D)),
        ],
        out_specs=pl.BlockSpec((tm, D), row),
        out_shape=jax.ShapeDtypeStruct((M, D), F32),
        compiler_params=_cparams(("parallel",)),
        name="merge",
    )(x2d, oa, ob, oc, proj, proj, proj, wa, wb, wc, wo)


def _ffn_kernel(x_ref, g_ref, win_ref, wout_ref, gf_ref, o_ref, *, dff, tf, final):
    x = x_ref[...]
    h = (x * lax.rsqrt(jnp.mean(x * x, axis=-1, keepdims=True) + EPS) * g_ref[...]).astype(BF16)
    acc = x
    for j in range(dff // tf):
        gg = jnp.dot(h, win_ref[:, j * tf:(j + 1) * tf], preferred_element_type=F32)
        uu = jnp.dot(h, win_ref[:, dff + j * tf:dff + (j + 1) * tf], preferred_element_type=F32)
        act = (_silu(gg) * uu).astype(BF16)
        acc = acc + jnp.dot(act, wout_ref[j * tf:(j + 1) * tf, :], preferred_element_type=F32)
    if final:
        acc = acc * lax.rsqrt(jnp.mean(acc * acc, axis=-1, keepdims=True) + EPS) * gf_ref[...]
    o_ref[...] = acc


def _ffn(x2d, g_all, win, wout, gf, l, tm, final):
    M, D = x2d.shape
    dff = wout.shape[1]
    tf = 256
    return pl.pallas_call(
        functools.partial(_ffn_kernel, dff=dff, tf=tf, final=final),
        grid=(M // tm,),
        in_specs=[
            pl.BlockSpec((tm, D), lambda i: (i, 0)),
            pl.BlockSpec((None, 1, D), lambda i: (l, 0, 0)),
            pl.BlockSpec((None, D, 2 * dff), lambda i: (l, 0, 0)),
            pl.BlockSpec((None, dff, D), lambda i: (l, 0, 0)),
            pl.BlockSpec((1, D), lambda i: (0, 0)),
        ],
        out_specs=pl.BlockSpec((tm, D), lambda i: (i, 0)),
        out_shape=jax.ShapeDtypeStruct((M, D), F32),
        compiler_params=_cparams(("parallel",)),
        name="ffn",
    )(x2d, g_all, win, wout, gf)


def _row_tile(M, cap):
    t = min(cap, M)
    while M % t:
        t //= 2
    return t


def _prep_weights(w_in, w_br_a, w_br_b, w_br_c, w_o, w_ffn_in, w_ffn_out):
    depth, D, _ = w_in.shape
    o_dnqkv = 3 * HD
    o_a = o_dnqkv + 3 * HD
    o_z = o_a + 2 * N_HEADS
    o_hg = o_z + HD
    o_gate = o_hg + 4 * HD
    pad = N_PROJ - (C_AB + 2 * N_HEADS)
    w_perm = jnp.concatenate([
        w_in[:, :, HD:3 * HD],
        w_in[:, :, 0:HD],
        w_in[:, :, o_dnqkv:o_a],
        w_in[:, :, o_gate:o_gate + 3 * D],
        w_in[:, :, o_z:o_hg],
        w_in[:, :, o_hg:o_gate],
        w_in[:, :, o_a:o_z],
        jnp.zeros((depth, D, pad), w_in.dtype),
    ], axis=-1).astype(BF16)
    bf = lambda a: a.astype(BF16)
    return w_perm, bf(w_br_a), bf(w_br_b), bf(w_br_c), bf(w_o), bf(w_ffn_in), bf(w_ffn_out)


def _pad_lanes(v, n=LANE):
    return jnp.pad(v, (0, n - v.shape[0]))


def _trunk(x, P, page_table, cache_k, cache_v, dn_conv, dn_S, hg_S):
    B, T, D = x.shape
    depth = P['ln1_g'].shape[0]
    M = B * T
    x2d = x.reshape(M, D)
    tm_proj = _row_tile(M, 1024)
    tm_tok = _row_tile(M, 512)
    W3 = 3 * HD
    dn_ns, dn_rs = _dn_tiling(B, T)
    dn_r = dn_ns * dn_rs
    assert T >= CONV_W - 1
    bufs, dSs, hSs = [], [], []
    kst = jnp.zeros((depth, M, HD), F32)
    vst = jnp.zeros((depth, M, HD), F32)
    lane_pad = lambda a, n: jnp.pad(a, ((0, 0), (0, n - a.shape[1])))
    alog_row = lane_pad(P['dn_A_log'], LANE)[:, None, :]
    dtb_row = lane_pad(P['dn_dt_bias'], LANE)[:, None, :]
    alog_col = jnp.broadcast_to(lane_pad(P['dn_A_log'], 8)[:, :, None], (depth, 8, dn_r))
    dtb_col = jnp.broadcast_to(lane_pad(P['dn_dt_bias'], 8)[:, :, None], (depth, 8, dn_r))
    dn_ng = P['dn_norm_g'][:, None, :]
    hg_ng = P['hg_norm_g'][:, None, :]
    if page_table is None:
        conv0 = jnp.zeros((1, B, 8, W3), F32)
        Sb0 = jnp.zeros((1, B, N_HEADS, HEAD_DIM, HEAD_DIM), F32)
        Sc0 = Sb0
    else:
        conv0 = jnp.pad(dn_conv, ((0, 0), (0, 0), (8 - (CONV_W - 1), 0), (0, 0)))
        Sb0, Sc0 = dn_S, hg_S
        rows = N_HEADS * T
        bias_rows = jnp.broadcast_to(jnp.repeat(P['sb_bias'], T, axis=1)[:, :, None],
                                     (depth, rows, LANE)).astype(F32)
    for l in range(depth):
        ls = 0 if page_table is None else l
        proj, kst, vst = _inproj(x2d, P['ln1_g3'], P['w_perm'], kst, vst, l, tm_proj)
        buf_new = proj.reshape(B, T, N_PROJ)[:, T - (CONV_W - 1):, C_DNQKV:C_DNQKV + W3]
        if page_table is None:
            o_a = _sb_prompt(proj, P['sb_bias'], l, B, T)
        else:
            o_a = _sb_decode(proj, cache_k, cache_v, page_table, bias_rows, l, B, T)
        abT = proj[:, C_AB:C_AB + 8].T
        o_b, Sb = _deltanet(proj, abT, conv0, Sb0, P['dn_conv_w'], alog_row, dtb_row, alog_col,
                            dtb_col, dn_ng, l, ls, B, T)
        o_c, Sc = _hgrn2(proj, Sc0, P['hg_lb_logits'], hg_ng, l, ls, B, T)
        x2d = _merge(x2d, o_a, o_b, o_c, proj, P['w_br_a'], P['w_br_b'], P['w_br_c'], P['w_o'], l, tm_tok)
        x2d = _ffn(x2d, P['ln2_g3'], P['w_ffn_in'], P['w_ffn_out'], P['lnf_g'][None, :], l, tm_tok,
                   final=(l == depth - 1))
        bufs.append(buf_new)
        dSs.append(Sb)
        hSs.append(Sc)
    y = x2d.reshape(B, T, D)
    kv_shape = (depth, B, T, N_HEADS, HEAD_DIM)
    return (y, kst.reshape(kv_shape), vst.reshape(kv_shape), jnp.stack(bufs), jnp.stack(dSs),
            jnp.stack(hSs))


def kernel(x_prompt, x_sample, cache_sb_k, cache_sb_v, state_dn_conv, state_dn_S, state_hg_S, page_table,
           ln1_g, w_in, sb_bias, dn_conv_w, dn_A_log, dn_dt_bias, dn_norm_g, hg_lb_logits, hg_norm_g,
           w_br_a, w_br_b, w_br_c, w_o, ln2_g, w_ffn_in, w_ffn_out, lnf_g):
    w_perm, wa, wb, wc, wo, wfi, wfo = _prep_weights(w_in, w_br_a, w_br_b, w_br_c, w_o, w_ffn_in, w_ffn_out)
    P = dict(ln1_g3=ln1_g[:, None, :], ln2_g3=ln2_g[:, None, :], w_perm=w_perm, sb_bias=sb_bias,
             dn_conv_w=dn_conv_w, dn_A_log=dn_A_log, dn_dt_bias=dn_dt_bias, dn_norm_g=dn_norm_g,
             hg_lb_logits=hg_lb_logits.astype(F32), hg_norm_g=hg_norm_g, w_br_a=wa, w_br_b=wb, w_br_c=wc,
             w_o=wo, w_ffn_in=wfi, w_ffn_out=wfo, lnf_g=lnf_g, ln1_g=ln1_g)
    depth, n_pool = cache_sb_k.shape[:2]
    ck = cache_sb_k.reshape(depth, n_pool, PAGE_ROWS, HEAD_DIM)
    cv = cache_sb_v.reshape(depth, n_pool, PAGE_ROWS, HEAD_DIM)
    y_p, sbk_p, sbv_p, conv_p, dnS_p, hgS_p = _trunk(x_prompt, P, None, None, None, None, None, None)
    y_s, sbk_s, sbv_s, conv_s, dnS_s, hgS_s = _trunk(x_sample, P, page_table, ck, cv, state_dn_conv,
                                                     state_dn_S, state_hg_S)
    return (y_p, y_s, sbk_p, sbv_p, sbk_s, sbv_s, conv_p, conv_s, dnS_p, dnS_s, hgS_p, hgS_s)
```

```python
import functools
import math

import jax
import jax.numpy as jnp
from jax import lax
from jax.experimental import pallas as pl
from jax.experimental.pallas import tpu as pltpu

F32 = jnp.float32
BF16 = jnp.bfloat16
EPS = 1e-6
LOG2E = math.log2(math.e)

N_HEADS = 4
HEAD_DIM = 128
HD = N_HEADS * HEAD_DIM
CONV_W = 4
PAGE = 128
PAGE_ROWS = PAGE * N_HEADS
CHUNK = 64
SUB = 16
HG_SUB = 16
HG_DIAG = 4
SBP_STREAMS = 4
LANE = 128
DN_ROWS = 128
DN_STEP_ROWS = 256
VMEM_LIMIT = 56 * 1024 * 1024

C_SBK, C_SBV, C_SBQ = 0, 512, 1024
C_DNQKV = 1536
C_GATE = 3072
C_DNZ = 6144
C_HGQ, C_HGF, C_HGI, C_HGZ = 6656, 7168, 7680, 8192
C_AB = 8704
N_PROJ = 9216
PROJ_TN = 1536
DEC_PAGES = 16


def _sigmoid(x):
    return 1.0 / (1.0 + jnp.exp(-x))


def _silu(x):
    return x * _sigmoid(x)


def _softplus(x):
    return jnp.maximum(x, 0.0) + jnp.log(1.0 + jnp.exp(-jnp.abs(x)))


def _dot(a, b):
    return jnp.dot(a.astype(BF16), b.astype(BF16), preferred_element_type=F32)


def _dot_nt(a, b):
    return lax.dot_general(a.astype(BF16), b.astype(BF16), (((1,), (1,)), ((), ())),
                           preferred_element_type=F32)


def _dot_tn(a, b):
    return lax.dot_general(a.astype(BF16), b.astype(BF16), (((0,), (0,)), ((), ())),
                           preferred_element_type=F32)


def _split(a, n):
    out = []
    r = a
    for t in range(n):
        p = r.astype(BF16)
        out.append(p)
        if t + 1 < n:
            r = r - p.astype(F32)
    return out


def _split_trunc(a):
    bits = lax.bitcast_convert_type(a, jnp.uint32) & jnp.uint32(0xFFFF0000)
    hi = lax.bitcast_convert_type(bits, F32)
    return hi.astype(BF16), (a - hi).astype(BF16)


def _dot_exact_rhs(a, m_bf16, n=3):
    acc = None
    for p in _split(a, n):
        t = jnp.dot(p, m_bf16, preferred_element_type=F32)
        acc = t if acc is None else acc + t
    return acc


def _dot_exact_lhs(m_bf16, a, n=3):
    acc = None
    for p in _split(a, n):
        t = jnp.dot(m_bf16, p, preferred_element_type=F32)
        acc = t if acc is None else acc + t
    return acc


def _dot_hp(a, b, nt=False):
    a1, a2 = _split(a, 2)
    b1, b2 = _split(b, 2)
    if nt:
        f = lambda x, y: lax.dot_general(x, y, (((1,), (1,)), ((), ())), preferred_element_type=F32)
    else:
        f = lambda x, y: jnp.dot(x, y, preferred_element_type=F32)
    return f(a1, b1) + (f(a1, b2) + f(a2, b1))


def _iota2(shape):
    return (lax.broadcasted_iota(jnp.int32, shape, 0), lax.broadcasted_iota(jnp.int32, shape, 1))


def _log2(n):
    k = n.bit_length() - 1
    assert (1 << k) == n
    return k


def _cparams(sem):
    return pltpu.CompilerParams(dimension_semantics=sem, vmem_limit_bytes=VMEM_LIMIT)


def _inproj_kernel(x_ref, g_ref, w_ref, kin_ref, vin_ref, o_ref, kst_ref, vst_ref, h_scr):
    del kin_ref, vin_ref
    j = pl.program_id(1)

    @pl.when(j == 0)
    def _():
        x = x_ref[...]
        ms = jnp.mean(x * x, axis=-1, keepdims=True)
        h_scr[...] = (x * lax.rsqrt(ms + EPS) * g_ref[...]).astype(BF16)

    res = jnp.dot(h_scr[...], w_ref[...], preferred_element_type=F32)
    o_ref[...] = res

    @pl.when(j == 0)
    def _():
        kst_ref[...] = res[:, C_SBK:C_SBK + HD]
        vst_ref[...] = res[:, C_SBV:C_SBV + HD]


def _inproj(x2d, g_all, w_all, kst, vst, l, tm):
    M, D = x2d.shape
    depth = kst.shape[0]
    kv_spec = pl.BlockSpec((None, tm, HD), lambda i, j: (l, i, 0))
    return pl.pallas_call(
        _inproj_kernel,
        grid=(M // tm, N_PROJ // PROJ_TN),
        in_specs=[
            pl.BlockSpec((tm, D), lambda i, j: (i, 0)),
            pl.BlockSpec((None, 1, D), lambda i, j: (l, 0, 0)),
            pl.BlockSpec((None, D, PROJ_TN), lambda i, j: (l, 0, j)),
            pl.BlockSpec(memory_space=pl.ANY),
            pl.BlockSpec(memory_space=pl.ANY),
        ],
        out_specs=[pl.BlockSpec((tm, PROJ_TN), lambda i, j: (i, j)), kv_spec, kv_spec],
        out_shape=[jax.ShapeDtypeStruct((M, N_PROJ), F32),
                   jax.ShapeDtypeStruct((depth, M, HD), F32),
                   jax.ShapeDtypeStruct((depth, M, HD), F32)],
        input_output_aliases={3: 1, 4: 2},
        scratch_shapes=[pltpu.VMEM((tm, D), BF16)],
        compiler_params=_cparams(("parallel", "arbitrary")),
        name="inproj",
    )(x2d, g_all, w_all, kst, vst)


def _log2_sig_pair(z2):
    lg = jnp.log2(1.0 + jnp.exp2(-jnp.abs(z2)))
    ls = jnp.minimum(z2, 0.0) - lg
    return ls, ls - z2


def _sbp_blocks(qs, k, v, bias2, U, acc, csum, streams, valids):
    z2 = [_dot_nt(qs[s], k) + bias2 for s in streams]
    pairs = [_log2_sig_pair(z) for z in z2]
    l1ms = [p[1] if m is None else jnp.where(m, p[1], 0.0) for p, m in zip(pairs, valids)]
    splits = [_split_trunc(l) for l in l1ms]
    sufs = [jnp.dot(jnp.concatenate([hi, lo], axis=1), U, preferred_element_type=F32) for hi, lo in splits]
    ws = [jnp.exp2(p[0] + suf + csum[s]) for p, suf, s in zip(pairs, sufs, streams)]
    ws = [w if m is None else jnp.where(m, w, 0.0) for w, m in zip(ws, valids)]
    for w, l, s in zip(ws, l1ms, streams):
        acc[s] += _dot(w, v)
        csum[s] += jnp.sum(l, axis=-1, keepdims=True)


def _sbp_kernel(bias_ref, q_ref, k_ref, v_ref, u_ref, o_ref, kbf, vbf, acc, csum, *, tq, nstr, layer):
    h = pl.program_id(1)
    i = pl.program_id(2)

    @pl.when(i == 0)
    def _():
        kbf[...] = k_ref[...].astype(BF16)
        vbf[...] = v_ref[...].astype(BF16)

    bias2 = bias_ref[layer, h] * LOG2E
    q_all = (q_ref[...] * (HEAD_DIM ** -0.5 * LOG2E)).astype(BF16)
    qs = [q_all[s * tq:(s + 1) * tq, :] for s in range(nstr)]
    U = u_ref[...]
    acc[...] = jnp.zeros_like(acc)
    csum[...] = jnp.zeros_like(csum)

    row, col = _iota2((tq, tq))
    causal = col < row

    def kv(kb):
        st = pl.multiple_of(kb * tq, tq)
        return kbf[pl.ds(st, tq), :], vbf[pl.ds(st, tq), :]

    for t in range(nstr):
        d = nstr - 1 - t
        k, v = kv(nstr * i + d)
        streams = list(range(d, nstr))
        _sbp_blocks(qs, k, v, bias2, U, acc, csum, streams, [causal] + [None] * (len(streams) - 1))

    def body(jj, carry):
        k, v = kv(nstr * i - 1 - jj)
        _sbp_blocks(qs, k, v, bias2, U, acc, csum, list(range(nstr)), [None] * nstr)
        return carry

    lax.fori_loop(0, nstr * i, body, 0)
    for s in range(nstr):
        o_ref[s * tq:(s + 1) * tq, :] = acc[s]


def _sb_prompt(proj, bias_all, l, B, T):
    tq = min(256, T)
    nstr = SBP_STREAMS
    while T % (nstr * tq):
        nstr //= 2
    tqs = nstr * tq
    nq = T // tqs
    r = jnp.arange(tq)
    U1 = (r[:, None] > r[None, :]).astype(BF16)
    U = jnp.concatenate([U1, U1], axis=0)
    return pl.pallas_call(
        functools.partial(_sbp_kernel, tq=tq, nstr=nstr, layer=l),
        grid=(B, N_HEADS, nq),
        in_specs=[
            pl.BlockSpec(memory_space=pltpu.SMEM),
            pl.BlockSpec((tqs, HEAD_DIM), lambda b, h, i: (b * nq + i, C_SBQ // HEAD_DIM + h)),
            pl.BlockSpec((T, HEAD_DIM), lambda b, h, i: (b, C_SBK // HEAD_DIM + h)),
            pl.BlockSpec((T, HEAD_DIM), lambda b, h, i: (b, C_SBV // HEAD_DIM + h)),
            pl.BlockSpec((2 * tq, tq), lambda b, h, i: (0, 0)),
        ],
        out_specs=pl.BlockSpec((tqs, HEAD_DIM), lambda b, h, i: (b * nq + i, h)),
        out_shape=jax.ShapeDtypeStruct((B * T, HD), F32),
        scratch_shapes=[
            pltpu.VMEM((T, HEAD_DIM), BF16),
            pltpu.VMEM((T, HEAD_DIM), BF16),
            pltpu.VMEM((nstr, tq, HEAD_DIM), F32),
            pltpu.VMEM((nstr, tq, 1), F32),
        ],
        compiler_params=_cparams(("parallel", "parallel", "arbitrary")),
        name="sb_prompt",
    )(bias_all, proj, proj, proj, U)


def _sbd_group(qr, ks, vs, bias2, U, hm, extra_valid, acc_ref, csum_ref):
    n = len(ks)
    nstr = min(2, n)
    per = n // nstr
    rows = qr.shape[0]
    cat = lambda xs, ax: xs[0] if len(xs) == 1 else jnp.concatenate(xs, axis=ax)
    valid = hm if extra_valid is None else extra_valid
    valid_all = cat([valid] * per, 1)
    groups = [list(range(s * per, (s + 1) * per)) for s in range(nstr)]
    z2s = [_dot_nt(qr, cat([ks[p] for p in g], 0)) + bias2 for g in groups]
    pairs = [_log2_sig_pair(z) for z in z2s]
    l1ms = [jnp.where(valid_all, p[1], 0.0) for p in pairs]
    pages = [[l[:, p * PAGE_ROWS:(p + 1) * PAGE_ROWS] for p in range(per)] for l in l1ms]
    c = csum_ref[...]
    carries = []
    for s in range(nstr):
        cs = []
        for p in range(per):
            cs.append(c)
            c = c + jnp.sum(pages[s][p], axis=-1, keepdims=True)
        carries.append(cs)
    csum_ref[...] = c
    splits = [_split(cat(pg, 0), 2) for pg in pages]
    sufs = [jnp.dot(hi, U, preferred_element_type=F32) + jnp.dot(lo, U, preferred_element_type=F32)
            for hi, lo in splits]
    out = None
    for s in range(nstr):
        ws = []
        for p in range(per):
            e = (pairs[s][0][:, p * PAGE_ROWS:(p + 1) * PAGE_ROWS] + sufs[s][p * rows:(p + 1) * rows, :]
                 + carries[s][p])
            ws.append(jnp.where(valid, jnp.exp2(e), 0.0).astype(BF16))
        t = jnp.dot(cat(ws, 1), cat([vs[p] for p in groups[s]], 0), preferred_element_type=F32)
        out = t if out is None else out + t
    acc_ref[...] += out


def _sbd_kernel(pt_ref, bias_ref, q_ref, kn_ref, vn_ref, *rest, G, n_groups, ts):
    kpages = rest[:G]
    vpages = rest[G:2 * G]
    u_ref, o_ref, qr, kbuf, vbuf, acc, csum = rest[2 * G:]
    del pt_ref
    g = pl.program_id(1)
    rows = N_HEADS * ts
    U = u_ref[...]
    bias2 = bias_ref[:, 0:1] * LOG2E
    row, col = _iota2((rows, PAGE_ROWS))
    hm = (col & (N_HEADS - 1)) == (row // ts)

    @pl.when(g == 0)
    def _():
        acc[...] = jnp.zeros_like(acc)
        csum[...] = jnp.zeros_like(csum)
        q = q_ref[...] * (HEAD_DIM ** -0.5 * LOG2E)
        for hh in range(N_HEADS):
            qr[hh * ts:(hh + 1) * ts, :] = q[:, hh * HEAD_DIM:(hh + 1) * HEAD_DIM].astype(BF16)
        kbuf[...] = jnp.zeros_like(kbuf)
        vbuf[...] = jnp.zeros_like(vbuf)
        kbuf[0:ts * N_HEADS, :] = kn_ref[...].astype(BF16)
        vbuf[0:ts * N_HEADS, :] = vn_ref[...].astype(BF16)
        valid_new = hm & ((col >> 2) < (row % ts))
        _sbd_group(qr[...], [kbuf[...]], [vbuf[...]], bias2, U, hm, valid_new, acc, csum)

    @pl.when(g > 0)
    def _():
        _sbd_group(qr[...], [r[...].astype(BF16) for r in kpages], [r[...].astype(BF16) for r in vpages],
                   bias2, U, hm, None, acc, csum)

    @pl.when(g == n_groups)
    def _():
        for hh in range(N_HEADS):
            o_ref[:, hh * HEAD_DIM:(hh + 1) * HEAD_DIM] = acc[hh * ts:(hh + 1) * ts, :]


def _sb_decode(proj, cache_k, cache_v, page_table, bias_rows, l, nb, ts):
    npg = page_table.shape[1]
    G = min(DEC_PAGES, npg)
    n_groups = npg // G
    rows = N_HEADS * ts
    r = jnp.arange(PAGE_ROWS)
    U = (((r[:, None] & 3) == (r[None, :] & 3)) & (r[:, None] > r[None, :])).astype(BF16)
    kn = proj[:, C_SBK:C_SBK + HD].reshape(nb * ts * N_HEADS, HEAD_DIM)
    vn = proj[:, C_SBV:C_SBV + HD].reshape(nb * ts * N_HEADS, HEAD_DIM)

    def page_spec(p):
        def imap(b, g, pt):
            first = jnp.maximum(g - 1, 0) * G
            return (l, pt[b, npg - 1 - (first + p)], 0, 0)
        return pl.BlockSpec((None, None, PAGE_ROWS, HEAD_DIM), imap)

    in_specs = [
        pl.BlockSpec((None, rows, LANE), lambda b, g, pt: (l, 0, 0)),
        pl.BlockSpec((ts, HD), lambda b, g, pt: (b, C_SBQ // HD)),
        pl.BlockSpec((rows, HEAD_DIM), lambda b, g, pt: (b, 0)),
        pl.BlockSpec((rows, HEAD_DIM), lambda b, g, pt: (b, 0)),
    ] + [page_spec(p) for p in range(G)] + [page_spec(p) for p in range(G)] + [
        pl.BlockSpec((PAGE_ROWS, PAGE_ROWS), lambda b, g, pt: (0, 0)),
    ]
    grid_spec = pltpu.PrefetchScalarGridSpec(
        num_scalar_prefetch=1,
        grid=(nb, n_groups + 1),
        in_specs=in_specs,
        out_specs=pl.BlockSpec((ts, HD), lambda b, g, pt: (b, 0)),
        scratch_shapes=[
            pltpu.VMEM((rows, HEAD_DIM), BF16),
            pltpu.VMEM((PAGE_ROWS, HEAD_DIM), BF16),
            pltpu.VMEM((PAGE_ROWS, HEAD_DIM), BF16),
            pltpu.VMEM((rows, HEAD_DIM), F32),
            pltpu.VMEM((rows, 1), F32),
        ],
    )
    return pl.pallas_call(
        functools.partial(_sbd_kernel, G=G, n_groups=n_groups, ts=ts),
        grid_spec=grid_spec,
        out_shape=jax.ShapeDtypeStruct((nb * ts, HD), F32),
        compiler_params=_cparams(("parallel", "arbitrary")),
        name="sb_decode",
    )(page_table, bias_rows, proj, kn, vn, *([cache_k] * G), *([cache_v] * G), U)


def _unit_lower_inverse(As, C, row, col):
    eye = (row == col).astype(F32)
    blk = min(SUB, C)
    sh = _log2(blk)
    same = (row >> sh) == (col >> sh)
    Ms = [-jnp.where(same, A, 0.0) for A in As]
    Rs = [eye + M for M in Ms]
    for _ in range(sh - 1):
        Ms = [_dot_hp(M, M) for M in Ms]
        Rs = [R + _dot_hp(R, M) for R, M in zip(Rs, Ms)]
    if C > blk:
        Ps = [-_dot_hp(R, jnp.where(same, 0.0, A)) for R, A in zip(Rs, As)]
        Xs = [eye + P for P in Ps]
        for _ in range(_log2(C // blk) - 1):
            Ps = [_dot_hp(P, P) for P in Ps]
            Xs = [X + _dot_hp(X, P) for X, P in zip(Xs, Ps)]
        Rs = [_dot_hp(X, R) for X, R in zip(Xs, Rs)]
    return Rs


def _dn_kernel(qkv_ref, z_ref, ab_ref, abT_ref, conv0_ref, s0_ref, convw_ref, alog_row_ref,
               dtb_row_ref, alog_col_ref, dtb_col_ref, ng_ref, o_ref, sout_ref, xbuf, s_scr,
               *, ns, rs, C, nsteps):
    c = pl.program_id(1)
    R = ns * rs
    npc = rs // C

    @pl.when(c == 0)
    def _():
        xbuf[:, 0:8, :] = conv0_ref[...]
        s_scr[...] = s0_ref[...]

    @pl.when(c > 0)
    def _():
        xbuf[:, 0:8, :] = xbuf[:, rs:rs + 8, :]

    ys = []
    for j in range(ns):
        xbuf[j, 8:8 + rs, :] = qkv_ref[j * rs:(j + 1) * rs, :]
        acc = xbuf[j, 5:5 + rs, :] * convw_ref[0:1, :]
        for t in range(1, CONV_W):
            acc = acc + xbuf[j, 5 + t:5 + t + rs, :] * convw_ref[t:t + 1, :]
        ys.append(acc)
    y = _silu(ys[0] if ns == 1 else jnp.concatenate(ys, axis=0))

    BR = min(R, DN_ROWS)
    nblk = R // BR
    row, col = _iota2((BR, BR))
    shc = _log2(C)
    same_chunk = (row >> shc) == (col >> shc)
    lower_incl = same_chunk & (row >= col)
    tri_lower = lower_incl.astype(BF16)
    tri_upper = (same_chunk & (row <= col)).astype(BF16)
    heads = range(N_HEADS)
    items = [(b, h) for b in range(nblk) for h in heads]
    it = {bh: n for n, bh in enumerate(items)}

    beta_blk, gcol_blk, grow_blk = [], [], []
    for b in range(nblk):
        ab = ab_ref[b * BR:(b + 1) * BR, :]
        g_col = -jnp.exp(alog_row_ref[...]) * _softplus(ab + dtb_row_ref[...])
        beta_blk.append(_sigmoid(ab))
        gcol_blk.append(_dot_exact_lhs(tri_lower, g_col))
        abT = abT_ref[:, b * BR:(b + 1) * BR]
        g_row = -jnp.exp(alog_col_ref[:, 0:BR]) * _softplus(abT + dtb_col_ref[:, 0:BR])
        grow_blk.append(_dot_exact_rhs(g_row, tri_upper))

    ng = ng_ref[...]
    qs, kls, vs, betas, gcs, decays, kbs = [], [], [], [], [], [], []
    for b, h in items:
        yb = y[b * BR:(b + 1) * BR, :]
        q = yb[:, h * HEAD_DIM:(h + 1) * HEAD_DIM]
        k = yb[:, HD + h * HEAD_DIM:HD + (h + 1) * HEAD_DIM]
        vs.append(yb[:, 2 * HD + h * HEAD_DIM:2 * HD + (h + 1) * HEAD_DIM])
        qs.append(q * lax.rsqrt(jnp.sum(q * q, axis=-1, keepdims=True) + EPS) * (HEAD_DIM ** -0.5))
        kls.append(k * lax.rsqrt(jnp.sum(k * k, axis=-1, keepdims=True) + EPS))
        betas.append(beta_blk[b][:, N_HEADS + h:N_HEADS + h + 1])
        gcs.append(gcol_blk[b][:, h:h + 1])
        gr = grow_blk[b][h:h + 1, :]
        decays.append(jnp.exp(jnp.where(lower_incl, gcs[-1] - gr, -jnp.inf)))
        kbs.append(kls[-1] * betas[-1])
    nit = range(len(items))
    As = [jnp.where(row > col, _dot_hp(kbs[n], kls[n], nt=True) * decays[n], 0.0) for n in nit]
    Tms = _unit_lower_inverse(As, C, row, col)
    egs = [jnp.exp(gcs[n]) for n in nit]
    sols = [_dot_hp(Tms[n], jnp.concatenate([vs[n] * betas[n], kbs[n] * egs[n]], axis=-1)) for n in nit]
    us = [s[:, :HEAD_DIM] for s in sols]
    ws = [s[:, HEAD_DIM:] for s in sols]
    attns = [(_dot_nt(qs[n], kls[n]) * decays[n]).astype(BF16) for n in nit]
    qgs = [qs[n] * egs[n] for n in nit]
    v_news = [[] for _ in nit]
    o_inters = [[] for _ in nit]
    for j in range(ns):
        Ss = [s_scr[j, h] for h in heads]
        for i in range(npc):
            g0 = j * rs + i * C
            b, r0 = g0 // BR, g0 % BR
            rows = slice(r0, r0 + C)
            for h in heads:
                n = it[(b, h)]
                v_new = us[n][rows, :] - _dot(ws[n][rows, :], Ss[h])
                o_inters[n].append(_dot(qgs[n][rows, :], Ss[h]))
                g_last = gcs[n][r0 + C - 1:r0 + C, :]
                kg = kls[n][rows, :] * jnp.exp(g_last - gcs[n][rows, :])
                Ss[h] = Ss[h] * jnp.exp(g_last) + _dot_tn(kg, v_new)
                v_news[n].append(v_new)
        for h in heads:
            s_scr[j, h] = Ss[h]
    for n, (b, h) in enumerate(items):
        hs = slice(h * HEAD_DIM, (h + 1) * HEAD_DIM)
        rb = slice(b * BR, (b + 1) * BR)
        v_new_all = v_news[n][0] if len(v_news[n]) == 1 else jnp.concatenate(v_news[n], axis=0)
        o_inter = o_inters[n][0] if len(o_inters[n]) == 1 else jnp.concatenate(o_inters[n], axis=0)
        o = o_inter + _dot(attns[n], v_new_all)
        o = o * lax.rsqrt(jnp.mean(o * o, axis=-1, keepdims=True) + EPS) * ng
        o_ref[rb, hs] = o * _silu(z_ref[rb, hs])

    @pl.when(c == nsteps - 1)
    def _():
        sout_ref[...] = s_scr[...]


def _dn_tiling(B, T):
    if T >= DN_ROWS:
        return 1, (DN_STEP_ROWS if T % DN_STEP_ROWS == 0 else DN_ROWS)
    return min(B, DN_ROWS // T), T


def _deltanet(proj, abT, conv0, S0, convw, alog_row, dtb_row, alog_col, dtb_col, ng, l, ls, B, T):
    C = CHUNK if T % CHUNK == 0 else T
    ns, rs = _dn_tiling(B, T)
    assert T % rs == 0 and rs % C == 0 and B % ns == 0
    layer = lambda b, c: (l, 0, 0)
    R = ns * rs
    nsteps = T // rs
    W3 = 3 * HD
    const = lambda b, c: (0, 0)
    return pl.pallas_call(
        functools.partial(_dn_kernel, ns=ns, rs=rs, C=C, nsteps=nsteps),
        grid=(B // ns, nsteps),
        in_specs=[
            pl.BlockSpec((R, W3), lambda b, c: (b * nsteps + c, C_DNQKV // W3)),
            pl.BlockSpec((R, HD), lambda b, c: (b * nsteps + c, C_DNZ // HD)),
            pl.BlockSpec((R, LANE), lambda b, c: (b * nsteps + c, C_AB // LANE)),
            pl.BlockSpec((8, R), lambda b, c: (0, b * nsteps + c)),
            pl.BlockSpec((None, ns, 8, W3), lambda b, c: (ls, b, 0, 0)),
            pl.BlockSpec((None, ns, N_HEADS, HEAD_DIM, HEAD_DIM), lambda b, c: (ls, b, 0, 0, 0)),
            pl.BlockSpec((None, CONV_W, W3), layer),
            pl.BlockSpec((None, 1, LANE), layer),
            pl.BlockSpec((None, 1, LANE), layer),
            pl.BlockSpec((None, 8, R), layer),
            pl.BlockSpec((None, 8, R), layer),
            pl.BlockSpec((None, 1, HEAD_DIM), layer),
        ],
        out_specs=[
            pl.BlockSpec((R, HD), lambda b, c: (b * nsteps + c, 0)),
            pl.BlockSpec((ns, N_HEADS, HEAD_DIM, HEAD_DIM), lambda b, c: (b, 0, 0, 0)),
        ],
        out_shape=[
            jax.ShapeDtypeStruct((B * T, HD), F32),
            jax.ShapeDtypeStruct((B, N_HEADS, HEAD_DIM, HEAD_DIM), F32),
        ],
        scratch_shapes=[
            pltpu.VMEM((ns, 8 + rs, W3), F32),
            pltpu.VMEM((ns, N_HEADS, HEAD_DIM, HEAD_DIM), F32),
        ],
        compiler_params=_cparams(("parallel", "arbitrary")),
        name="deltanet",
    )(proj, proj, proj, abT, conv0, S0, convw, alog_row, dtb_row, alog_col, dtb_col, ng)


def _hg_refs(g, s, C):
    if s % 8 == 0:
        n = C // s
        first = jnp.broadcast_to(g.reshape(n, s, HD)[:, 0:1, :], (n, s, HD)).reshape(C, HD)
        return first, jnp.concatenate([first[s:, :], first[C - s:, :]], axis=0)
    assert s == 4
    n = C // 8
    g3 = g.reshape(n, 8, HD)
    r0 = jnp.broadcast_to(g3[:, 0:1, :], (n, 8, HD)).reshape(C, HD)
    r4 = jnp.broadcast_to(g3[:, 4:5, :], (n, 8, HD)).reshape(C, HD)
    sub = lax.broadcasted_iota(jnp.int32, (C, HD), 0) & 7
    return jnp.where(sub < 4, r0, r4), r4


def _hg_kernel(q_ref, f_ref, i_ref, z_ref, s0_ref, logit_ref, ng_ref, o_ref, sout_ref,
               kpad, gpad, vpad, st_scr, *, C, nc, layer):
    c = pl.program_id(1)
    sb = min(HG_SUB, C)

    @pl.when(c == 0)
    def _():
        for h in range(N_HEADS):
            st_scr[h] = s0_ref[h].T
        kpad[0:sb, :] = jnp.zeros((sb, HD), F32)
        gpad[0:sb, :] = jnp.zeros((sb, HD), F32)
        vpad[0:sb, :] = jnp.zeros((sb, HD), F32)

    logits = logit_ref[...]
    e = jnp.exp(logits - jnp.max(logits, axis=0, keepdims=True))
    p = e / jnp.sum(e, axis=0, keepdims=True)
    lb = jnp.sum(p[0:layer + 1, :], axis=0, keepdims=True) - p[0:1, :]

    q = _silu(q_ref[...]) * (HEAD_DIM ** -0.5)
    fg = lb + (1.0 - lb) * _sigmoid(f_ref[...])
    k = 1.0 - fg
    v = i_ref[...]
    lf = jnp.log(fg)
    row, col = _iota2((C, C))
    gcum = _dot_exact_lhs((row >= col).astype(BF16), lf)

    kpad[sb:sb + C, :] = k
    gpad[sb:sb + C, :] = gcum
    vpad[sb:sb + C, :] = v

    sd = min(HG_DIAG, sb) if C > 8 else sb
    rpos = lax.broadcasted_iota(jnp.int32, (C, HD), 0) % sd
    o_diag = [jnp.zeros((C, HEAD_DIM), F32) for _ in range(N_HEADS)]
    for dl in range(sd):
        kd = kpad[sb - dl:sb - dl + C, :]
        gd = gpad[sb - dl:sb - dl + C, :]
        vd = vpad[sb - dl:sb - dl + C, :]
        ex = jnp.exp(jnp.where(rpos >= dl, gcum - gd, -jnp.inf))
        t = q * kd * ex
        for h in range(N_HEADS):
            hs = slice(h * HEAD_DIM, (h + 1) * HEAD_DIM)
            s = jnp.sum(t[:, hs], axis=-1, keepdims=True)
            o_diag[h] = o_diag[h] + s * vd[:, hs]

    eg = jnp.exp(gcum)
    g_last = gcum[C - 1:C, :]
    k_tail = k * jnp.exp(g_last - gcum)
    ng = ng_ref[...]
    nsb = C // sb
    heads = range(N_HEADS)
    hsl = [slice(h * HEAD_DIM, (h + 1) * HEAD_DIM) for h in heads]
    STs = [st_scr[h] for h in heads]
    os_ = [_dot_nt(q[:, hsl[h]] * eg[:, hsl[h]], STs[h]) + o_diag[h] for h in heads]
    for h in heads:
        st_scr[h] = STs[h] * jnp.exp(g_last[:, hsl[h]]) + _dot_tn(v[:, hsl[h]], k_tail[:, hsl[h]])
    attn = [jnp.zeros((C, C), F32) for _ in heads]
    s = sb // 2
    while s >= sd:
        gfirst, gnext = _hg_refs(gcum, s, C)
        qt = q * jnp.exp(gcum - gfirst)
        kt = k * jnp.exp(jnp.minimum(gnext - gcum, 0.0))
        sh = _log2(s)
        pair = (((row >> (sh + 1)) == (col >> (sh + 1)))
                & (((row >> sh) & 1) == 1) & (((col >> sh) & 1) == 0))
        attn = [attn[h] + jnp.where(pair, _dot_nt(qt[:, hsl[h]], kt[:, hsl[h]]), 0.0) for h in heads]
        s //= 2
    if nsb > 1:
        gblk = jnp.broadcast_to(gcum.reshape(nsb, sb, HD)[:, 0:1, :], (nsb, sb, HD)).reshape(C, HD)
        qt = q * jnp.exp(gcum - gblk)
        shb = _log2(sb)
        for I in range(1, nsb):
            gn = gcum[I * sb:I * sb + 1, :]
            kt = k * jnp.exp(jnp.minimum(gn - gcum, 0.0))
            sel = ((row >> shb) == I) & (col < I * sb)
            attn = [attn[h] + jnp.where(sel, _dot_nt(qt[:, hsl[h]], kt[:, hsl[h]]), 0.0) for h in heads]
    if nsb > 1 or sb > sd:
        os_ = [os_[h] + _dot(attn[h], v[:, hsl[h]]) for h in heads]
    for h in heads:
        o = os_[h]
        o = o * lax.rsqrt(jnp.mean(o * o, axis=-1, keepdims=True) + EPS) * ng
        o_ref[:, hsl[h]] = o * _silu(z_ref[:, hsl[h]])

    @pl.when(c == nc - 1)
    def _():
        for h in range(N_HEADS):
            sout_ref[h] = st_scr[h].T


def _hgrn2(proj, S0, logits, ng, l, ls, B, T):
    C = CHUNK if T % CHUNK == 0 else T
    nc = T // C
    sb = min(HG_SUB, C)
    depth = logits.shape[0]
    const = lambda b, c: (0, 0)
    return pl.pallas_call(
        functools.partial(_hg_kernel, C=C, nc=nc, layer=l),
        grid=(B, nc),
        in_specs=[
            pl.BlockSpec((C, HD), lambda b, c: (b * nc + c, C_HGQ // HD)),
            pl.BlockSpec((C, HD), lambda b, c: (b * nc + c, C_HGF // HD)),
            pl.BlockSpec((C, HD), lambda b, c: (b * nc + c, C_HGI // HD)),
            pl.BlockSpec((C, HD), lambda b, c: (b * nc + c, C_HGZ // HD)),
            pl.BlockSpec((None, None, N_HEADS, HEAD_DIM, HEAD_DIM), lambda b, c: (ls, b, 0, 0, 0)),
            pl.BlockSpec((depth, HD), const),
            pl.BlockSpec((None, 1, HEAD_DIM), lambda b, c: (l, 0, 0)),
        ],
        out_specs=[
            pl.BlockSpec((C, HD), lambda b, c: (b * nc + c, 0)),
            pl.BlockSpec((None, N_HEADS, HEAD_DIM, HEAD_DIM), lambda b, c: (b, 0, 0, 0)),
        ],
        out_shape=[
            jax.ShapeDtypeStruct((B * T, HD), F32),
            jax.ShapeDtypeStruct((B, N_HEADS, HEAD_DIM, HEAD_DIM), F32),
        ],
        scratch_shapes=[
            pltpu.VMEM((sb + C, HD), F32),
            pltpu.VMEM((sb + C, HD), F32),
            pltpu.VMEM((sb + C, HD), F32),
            pltpu.VMEM((N_HEADS, HEAD_DIM, HEAD_DIM), F32),
        ],
        compiler_params=_cparams(("parallel", "arbitrary")),
        name="hgrn2",
    )(proj, proj, proj, proj, S0, logits, ng)


def _merge_kernel(x_ref, oa_ref, ob_ref, oc_ref, ga_ref, gb_ref, gc_ref, wa_ref, wb_ref, wc_ref,
                  wo_ref, o_ref):
    mixed = (_sigmoid(ga_ref[...]) * _dot(oa_ref[...], wa_ref[...])
             + _sigmoid(gb_ref[...]) * _dot(ob_ref[...], wb_ref[...])
             + _sigmoid(gc_ref[...]) * _dot(oc_ref[...], wc_ref[...]))
    o_ref[...] = x_ref[...] + _dot(mixed, wo_ref[...])


def _merge(x2d, oa, ob, oc, proj, wa, wb, wc, wo, l, tm):
    M, D = x2d.shape
    row = lambda i: (i, 0)
    wspec = lambda shape: pl.BlockSpec((None,) + shape, lambda i: (l, 0, 0))
    g0 = C_GATE // D
    return pl.pallas_call(
        _merge_kernel,
        grid=(M // tm,),
        in_specs=[
            pl.BlockSpec((tm, D), row),
            pl.BlockSpec((tm, HD), row), pl.BlockSpec((tm, HD), row), pl.BlockSpec((tm, HD), row),
            pl.BlockSpec((tm, D), lambda i: (i, g0)),
            pl.BlockSpec((tm, D), lambda i: (i, g0 + 1)),
            pl.BlockSpec((tm, D), lambda i: (i, g0 + 2)),
            wspec((HD, D)), wspec((HD, D)), wspec((HD, D)), wspec((D, D)),
        ],
        out_specs=pl.BlockSpec((tm, D), row),
        out_shape=jax.ShapeDtypeStruct((M, D), F32),
        compiler_params=_cparams(("parallel",)),
        name="merge",
    )(x2d, oa, ob, oc, proj, proj, proj, wa, wb, wc, wo)


def _ffn_kernel(x_ref, g_ref, win_ref, wout_ref, gf_ref, o_ref, *, dff, tf, final):
    x = x_ref[...]
    h = (x * lax.rsqrt(jnp.mean(x * x, axis=-1, keepdims=True) + EPS) * g_ref[...]).astype(BF16)
    acc = x
    for j in range(dff // tf):
        gg = jnp.dot(h, win_ref[:, j * tf:(j + 1) * tf], preferred_element_type=F32)
        uu = jnp.dot(h, win_ref[:, dff + j * tf:dff + (j + 1) * tf], preferred_element_type=F32)
        act = (_silu(gg) * uu).astype(BF16)
        acc = acc + jnp.dot(act, wout_ref[j * tf:(j + 1) * tf, :], preferred_element_type=F32)
    if final:
        acc = acc * lax.rsqrt(jnp.mean(acc * acc, axis=-1, keepdims=True) + EPS) * gf_ref[...]
    o_ref[...] = acc


def _ffn(x2d, g_all, win, wout, gf, l, tm, final):
    M, D = x2d.shape
    dff = wout.shape[1]
    tf = 256
    return pl.pallas_call(
        functools.partial(_ffn_kernel, dff=dff, tf=tf, final=final),
        grid=(M // tm,),
        in_specs=[
            pl.BlockSpec((tm, D), lambda i: (i, 0)),
            pl.BlockSpec((None, 1, D), lambda i: (l, 0, 0)),
            pl.BlockSpec((None, D, 2 * dff), lambda i: (l, 0, 0)),
            pl.BlockSpec((None, dff, D), lambda i: (l, 0, 0)),
            pl.BlockSpec((1, D), lambda i: (0, 0)),
        ],
        out_specs=pl.BlockSpec((tm, D), lambda i: (i, 0)),
        out_shape=jax.ShapeDtypeStruct((M, D), F32),
        compiler_params=_cparams(("parallel",)),
        name="ffn",
    )(x2d, g_all, win, wout, gf)


def _row_tile(M, cap):
    t = min(cap, M)
    while M % t:
        t //= 2
    return t


def _prep_weights(w_in, w_br_a, w_br_b, w_br_c, w_o, w_ffn_in, w_ffn_out):
    depth, D, _ = w_in.shape
    o_dnqkv = 3 * HD
    o_a = o_dnqkv + 3 * HD
    o_z = o_a + 2 * N_HEADS
    o_hg = o_z + HD
    o_gate = o_hg + 4 * HD
    pad = N_PROJ - (C_AB + 2 * N_HEADS)
    w_perm = jnp.concatenate([
        w_in[:, :, HD:3 * HD],
        w_in[:, :, 0:HD],
        w_in[:, :, o_dnqkv:o_a],
        w_in[:, :, o_gate:o_gate + 3 * D],
        w_in[:, :, o_z:o_hg],
        w_in[:, :, o_hg:o_gate],
        w_in[:, :, o_a:o_z],
        jnp.zeros((depth, D, pad), w_in.dtype),
    ], axis=-1).astype(BF16)
    bf = lambda a: a.astype(BF16)
    return w_perm, bf(w_br_a), bf(w_br_b), bf(w_br_c), bf(w_o), bf(w_ffn_in), bf(w_ffn_out)


def _pad_lanes(v, n=LANE):
    return jnp.pad(v, (0, n - v.shape[0]))


def _trunk(x, P, page_table, cache_k, cache_v, dn_conv, dn_S, hg_S):
    B, T, D = x.shape
    depth = P['ln1_g'].shape[0]
    M = B * T
    x2d = x.reshape(M, D)
    tm_proj = _row_tile(M, 1024)
    tm_tok = _row_tile(M, 512)
    W3 = 3 * HD
    dn_ns, dn_rs = _dn_tiling(B, T)
    dn_r = dn_ns * dn_rs
    assert T >= CONV_W - 1
    bufs, dSs, hSs = [], [], []
    kst = jnp.zeros((depth, M, HD), F32)
    vst = jnp.zeros((depth, M, HD), F32)
    lane_pad = lambda a, n: jnp.pad(a, ((0, 0), (0, n - a.shape[1])))
    alog_row = lane_pad(P['dn_A_log'], LANE)[:, None, :]
    dtb_row = lane_pad(P['dn_dt_bias'], LANE)[:, None, :]
    alog_col = jnp.broadcast_to(lane_pad(P['dn_A_log'], 8)[:, :, None], (depth, 8, dn_r))
    dtb_col = jnp.broadcast_to(lane_pad(P['dn_dt_bias'], 8)[:, :, None], (depth, 8, dn_r))
    dn_ng = P['dn_norm_g'][:, None, :]
    hg_ng = P['hg_norm_g'][:, None, :]
    if page_table is None:
        conv0 = jnp.zeros((1, B, 8, W3), F32)
        Sb0 = jnp.zeros((1, B, N_HEADS, HEAD_DIM, HEAD_DIM), F32)
        Sc0 = Sb0
    else:
        conv0 = jnp.pad(dn_conv, ((0, 0), (0, 0), (8 - (CONV_W - 1), 0), (0, 0)))
        Sb0, Sc0 = dn_S, hg_S
        rows = N_HEADS * T
        bias_rows = jnp.broadcast_to(jnp.repeat(P['sb_bias'], T, axis=1)[:, :, None],
                                     (depth, rows, LANE)).astype(F32)
    for l in range(depth):
        ls = 0 if page_table is None else l
        proj, kst, vst = _inproj(x2d, P['ln1_g3'], P['w_perm'], kst, vst, l, tm_proj)
        buf_new = proj.reshape(B, T, N_PROJ)[:, T - (CONV_W - 1):, C_DNQKV:C_DNQKV + W3]
        if page_table is None:
            o_a = _sb_prompt(proj, P['sb_bias'], l, B, T)
        else:
            o_a = _sb_decode(proj, cache_k, cache_v, page_table, bias_rows, l, B, T)
        abT = proj[:, C_AB:C_AB + 8].T
        o_b, Sb = _deltanet(proj, abT, conv0, Sb0, P['dn_conv_w'], alog_row, dtb_row, alog_col,
                            dtb_col, dn_ng, l, ls, B, T)
        o_c, Sc = _hgrn2(proj, Sc0, P['hg_lb_logits'], hg_ng, l, ls, B, T)
        x2d = _merge(x2d, o_a, o_b, o_c, proj, P['w_br_a'], P['w_br_b'], P['w_br_c'], P['w_o'], l, tm_tok)
        x2d = _ffn(x2d, P['ln2_g3'], P['w_ffn_in'], P['w_ffn_out'], P['lnf_g'][None, :], l, tm_tok,
                   final=(l == depth - 1))
        bufs.append(buf_new)
        dSs.append(Sb)
        hSs.append(Sc)
    y = x2d.reshape(B, T, D)
    kv_shape = (depth, B, T, N_HEADS, HEAD_DIM)
    return (y, kst.reshape(kv_shape), vst.reshape(kv_shape), jnp.stack(bufs), jnp.stack(dSs),
            jnp.stack(hSs))


def kernel(x_prompt, x_sample, cache_sb_k, cache_sb_v, state_dn_conv, state_dn_S, state_hg_S, page_table,
           ln1_g, w_in, sb_bias, dn_conv_w, dn_A_log, dn_dt_bias, dn_norm_g, hg_lb_logits, hg_norm_g,
           w_br_a, w_br_b, w_br_c, w_o, ln2_g, w_ffn_in, w_ffn_out, lnf_g):
    w_perm, wa, wb, wc, wo, wfi, wfo = _prep_weights(w_in, w_br_a, w_br_b, w_br_c, w_o, w_ffn_in, w_ffn_out)
    P = dict(ln1_g3=ln1_g[:, None, :], ln2_g3=ln2_g[:, None, :], w_perm=w_perm, sb_bias=sb_bias,
             dn_conv_w=dn_conv_w, dn_A_log=dn_A_log, dn_dt_bias=dn_dt_bias, dn_norm_g=dn_norm_g,
             hg_lb_logits=hg_lb_logits.astype(F32), hg_norm_g=hg_norm_g, w_br_a=wa, w_br_b=wb, w_br_c=wc,
             w_o=wo, w_ffn_in=wfi, w_ffn_out=wfo, lnf_g=lnf_g, ln1_g=ln1_g)
    depth, n_pool = cache_sb_k.shape[:2]
    ck = cache_sb_k.reshape(depth, n_pool, PAGE_ROWS, HEAD_DIM)
    cv = cache_sb_v.reshape(depth, n_pool, PAGE_ROWS, HEAD_DIM)
    y_s, sbk_s, sbv_s, conv_s, dnS_s, hgS_s = _trunk(x_sample, P, page_table, ck, cv, state_dn_conv,
                                                     state_dn_S, state_hg_S)
    y_p, sbk_p, sbv_p, conv_p, dnS_p, hgS_p = _trunk(x_prompt, P, None, None, None, None, None, None)
    return (y_p, y_s, sbk_p, sbv_p, sbk_s, sbv_s, conv_p, conv_s, dnS_p, dnS_s, hgS_p, hgS_s)
```
